```python
import jax
import jax.numpy as jnp
from jax import lax
import numpy as np

D_MODEL = 1024
BATCH = 4
SEQ = 4096
DEPTH = 2
DEC_BATCH = 128
DEC_SEQ = 1
PAST_LEN = 2048
PAGE_SIZE = 128

N_EVEN = (DEPTH + 1) // 2
N_ODD = DEPTH // 2
HG_DIM = 128
HG_WIDTH = D_MODEL // 2
HG_HEADS = HG_WIDTH // HG_DIM
HG_CHUNK = 64
FOX_DIM = 64
FOX_WIDTH = D_MODEL - HG_WIDTH
FOX_HEADS = FOX_WIDTH // FOX_DIM
Q_BLOCK = 128
IN0_SIZES = (HG_WIDTH,) * 4 + (FOX_WIDTH,) * 3 + (FOX_HEADS,)
IN0_COLS = sum(IN0_SIZES)
IN0_SPLITS = tuple(sum(IN0_SIZES[:i + 1]) for i in range(len(IN0_SIZES) - 1))
CONV_CH = D_MODEL
CONV_W = 31
FFN_DIM = ((8 * D_MODEL // 3 + 127) // 128) * 128
FFN_CONV_W = 3
EPS = 1e-6

kernel_name = 'hybrid_hgrn2_fox_conformer_step'


def _rmsnorm(x, g):
    x32 = x.astype(jnp.float32)
    y = x32 * lax.rsqrt(jnp.mean(x32 * x32, axis=-1, keepdims=True) + EPS)
    return (y * g.astype(jnp.float32)).astype(x.dtype)


def _layernorm(x, g, b):
    x32 = x.astype(jnp.float32)
    xc = x32 - jnp.mean(x32, axis=-1, keepdims=True)
    y = xc * lax.rsqrt(jnp.mean(xc * xc, axis=-1, keepdims=True) + EPS)
    return y * g.astype(jnp.float32) + b.astype(jnp.float32)


def _dwconv(x_ext, w, b):
    c = x_ext.shape[-1]
    y = lax.conv_general_dilated(x_ext, w.astype(x_ext.dtype)[:, None, :], window_strides=(1,),
                                 padding='VALID', dimension_numbers=('NWC', 'WIO', 'NWC'),
                                 feature_group_count=c)
    return y + b.astype(y.dtype)


def _hgrn2_chunked(q, k, v, logf, s0):
    B, T, H, _ = q.shape
    DV = v.shape[-1]
    n = T // HG_CHUNK
    r = lambda a: a.reshape(B, n, HG_CHUNK, H, a.shape[-1])
    q, k, v, logf = r(q), r(k), r(v), r(logf)
    b = jnp.cumsum(logf, axis=2)
    b_last = b[:, :, -1]
    b_ref = b[:, :, HG_CHUNK // 2][:, :, None]
    kv = jnp.einsum('bnchk,bnchv->bnhkv', k * jnp.exp(b_last[:, :, None] - b), v)

    def step(s, inp):
        dec, kvc = inp
        return jnp.exp(dec)[..., None] * s + kvc, s

    s_fin, s_prev = lax.scan(step, s0, (jnp.moveaxis(b_last, 1, 0), jnp.moveaxis(kv, 1, 0)))
    s_prev = jnp.moveaxis(s_prev, 0, 1)
    o_inter = jnp.einsum('bnchk,bnhkv->bnchv', q * jnp.exp(b), s_prev)
    att = jnp.einsum('bnthk,bnshk->bnhts', q * jnp.exp(b - b_ref), k * jnp.exp(b_ref - b))
    causal = jnp.tril(jnp.ones((HG_CHUNK, HG_CHUNK), dtype=bool))
    att = jnp.where(causal, att, 0.0)
    o_intra = jnp.einsum('bnhts,bnshv->bnthv', att, v)
    return (o_inter + o_intra).reshape(B, T, H, DV), s_fin


def _hgrn2_recurrent(q, k, v, logf, s0):
    def step(s, inp):
        qt, kt, vt, lt = inp
        s = jnp.exp(lt)[..., None] * s + kt[..., None] * vt[..., None, :]
        return s, jnp.einsum('bhk,bhkv->bhv', qt, s)

    s_fin, o = lax.scan(step, s0, tuple(jnp.moveaxis(a, 1, 0) for a in (q, k, v, logf)))
    return jnp.moveaxis(o, 0, 1), s_fin


def _fox_prompt(q, k, v, logf):
    B, T, H, D = q.shape
    nb = T // Q_BLOCK
    scale = FOX_DIM ** -0.5
    c = jnp.cumsum(logf, axis=1).transpose(0, 2, 1)
    qb = jnp.moveaxis(q.reshape(B, nb, Q_BLOCK, H, D), 1, 0)
    cb = jnp.moveaxis(c.reshape(B, H, nb, Q_BLOCK), 2, 0)
    kpos = jnp.arange(T)

    def block(args):
        qi, ci, i = args
        s = jnp.einsum('bqhd,bkhd->bhqk', qi, k).astype(jnp.float32) * scale
        s = s + ci[..., :, None] - c[:, :, None, :]
        qpos = i * Q_BLOCK + jnp.arange(Q_BLOCK)
        s = jnp.where(qpos[:, None] >= kpos[None, :], s, -jnp.inf)
        p = jax.nn.softmax(s, axis=-1)
        return jnp.einsum('bhqk,bkhd->bqhd', p.astype(v.dtype), v)

    o = lax.map(block, (qb, cb, jnp.arange(nb)))
    return jnp.moveaxis(o, 0, 1).reshape(B, T, H, D)


def _fox_sample(q, k, v, logf, k_past, v_past, logf_past):
    scale = FOX_DIM ** -0.5
    n = q.shape[1]
    P = k_past.shape[1]
    cp = jnp.cumsum(logf_past.astype(jnp.float32), axis=1)
    suffix = (cp[:, -1:] - cp).transpose(0, 2, 1)
    cn = jnp.cumsum(logf, axis=1).transpose(0, 2, 1)
    s_past = (jnp.einsum('bqhd,bkhd->bhqk', q, k_past).astype(jnp.float32) * scale
              + cn[..., :, None] + suffix[..., None, :])
    s_new = (jnp.einsum('bqhd,bkhd->bhqk', q, k).astype(jnp.float32) * scale
             + cn[..., :, None] - cn[..., None, :])
    s_new = jnp.where(jnp.tril(jnp.ones((n, n), dtype=bool)), s_new, -jnp.inf)
    p = jax.nn.softmax(jnp.concatenate([s_past, s_new], axis=-1), axis=-1)
    o = (jnp.einsum('bhqk,bkhd->bqhd', p[..., :P].astype(v.dtype), v_past.astype(v.dtype))
         + jnp.einsum('bhqk,bkhd->bqhd', p[..., P:].astype(v.dtype), v))
    return o


def _even_mix(h, w_in, fox_fb, lb, gnorm, w_out, s0, past):
    B, T, _ = h.shape
    f32 = jnp.float32
    hq, hf, hi, hg, fq, fk, fv, ff = jnp.split(h @ w_in, IN0_SPLITS, axis=-1)
    fr = lb + (1.0 - lb) * jax.nn.sigmoid(hf.astype(f32))
    hh = lambda a: a.reshape(B, T, HG_HEADS, HG_DIM)
    q, k, v, lf = hh(jax.nn.silu(hq.astype(f32))), hh(1.0 - fr), hh(hi.astype(f32)), hh(jnp.log(fr))
    fh = lambda a: a.reshape(B, T, FOX_HEADS, FOX_DIM)
    fq, fk, fv = fh(fq), fh(fk), fh(fv)
    flf = jax.nn.log_sigmoid(ff.astype(f32) + fox_fb.astype(f32))
    if past is None:
        o_hg, s_fin = _hgrn2_chunked(q, k, v, lf, s0)
        o_fox = _fox_prompt(fq, fk, fv, flf)
    else:
        o_hg, s_fin = _hgrn2_recurrent(q, k, v, lf, s0)
        o_fox = _fox_sample(fq, fk, fv, flf, past[0], past[1], past[2])
    o_hg = o_hg * lax.rsqrt(jnp.mean(o_hg * o_hg, axis=-1, keepdims=True) + EPS)
    o_hg = o_hg.reshape(B, T, HG_WIDTH) * gnorm.astype(f32) * jax.nn.silu(hg.astype(f32))
    cat = jnp.concatenate([o_hg.astype(h.dtype), o_fox.reshape(B, T, FOX_WIDTH).astype(h.dtype)], axis=-1)
    return cat @ w_out, fk, fv, flf, s_fin


def _conformer_conv(h, hist, w_pw1, b_pw1, w_dw, b_dw, ln_g, ln_b, w_pw2, b_pw2):
    u = h @ w_pw1 + b_pw1
    a, g = jnp.split(u, 2, axis=-1)
    u = a * jax.nn.sigmoid(g)
    ext = jnp.concatenate([hist.astype(u.dtype), u], axis=1)
    d = _layernorm(_dwconv(ext, w_dw, b_dw), ln_g, ln_b)
    out = jax.nn.silu(d).astype(h.dtype) @ w_pw2 + b_pw2
    return out, ext[:, -(CONV_W - 1):]


def _conv_ffn(h, hist, w_in, w_dw, b_dw, w_out):
    g, u = jnp.split(h @ w_in, 2, axis=-1)
    ext = jnp.concatenate([hist.astype(g.dtype), g], axis=1)
    gc = _dwconv(ext, w_dw, b_dw)
    out = (jax.nn.gelu(gc.astype(jnp.float32), approximate=False) * u.astype(jnp.float32)).astype(h.dtype) @ w_out
    return out, ext[:, -(FFN_CONV_W - 1):]


def setup_inputs(seed: int = 0) -> dict:
    key = jax.random.key(seed)
    ks = iter(list(jax.random.split(key, 40)))
    f32 = jnp.float32
    nrm = lambda shape, scale: jax.random.normal(next(ks), shape, f32) * scale
    gain = lambda shape: 1.0 + 0.02 * jax.random.normal(next(ks), shape, f32)
    n_pages = PAST_LEN // PAGE_SIZE
    n_pool = (DEC_BATCH * n_pages * 5) // 4
    page_table = jax.random.permutation(next(ks), n_pool)[:DEC_BATCH * n_pages]
    page_table = page_table.reshape(DEC_BATCH, n_pages).astype(jnp.int32)
    logf_cache = jax.nn.log_sigmoid(
        jax.random.uniform(next(ks), (N_EVEN, n_pool, PAGE_SIZE, FOX_HEADS), f32, 1.0, 7.0))
    return {
        'x_prompt': nrm((BATCH, SEQ, D_MODEL), 1.0),
        'x_sample': nrm((DEC_BATCH, DEC_SEQ, D_MODEL), 1.0),
        'cache_fox_k': nrm((N_EVEN, n_pool, PAGE_SIZE, FOX_HEADS, FOX_DIM), 1.0),
        'cache_fox_v': nrm((N_EVEN, n_pool, PAGE_SIZE, FOX_HEADS, FOX_DIM), 1.0),
        'cache_fox_logf': logf_cache,
        'page_table': page_table,
        'state_hgrn': nrm((N_EVEN, DEC_BATCH, HG_HEADS, HG_DIM, HG_DIM), 1.0),
        'state_conv': nrm((N_ODD, DEC_BATCH, CONV_W - 1, CONV_CH), 0.5),
        'state_ffn_conv': nrm((DEPTH, DEC_BATCH, FFN_CONV_W - 1, FFN_DIM), 1.0),
        'norm_mix': gain((DEPTH, D_MODEL)),
        'norm_ffn': gain((DEPTH, D_MODEL)),
        'norm_final': gain((D_MODEL,)),
        'w_in0': nrm((N_EVEN, D_MODEL, IN0_COLS), D_MODEL ** -0.5),
        'fox_fb': jax.random.uniform(next(ks), (N_EVEN, FOX_HEADS), f32, 1.0, 7.0),
        'hg_lb': nrm((N_EVEN + 1, HG_WIDTH), 0.1),
        'hg_gnorm': gain((N_EVEN, HG_WIDTH)),
        'w_out0': nrm((N_EVEN, HG_WIDTH + FOX_WIDTH, D_MODEL), D_MODEL ** -0.5),
        'w_pw1': nrm((N_ODD, D_MODEL, 2 * CONV_CH), D_MODEL ** -0.5),
        'b_pw1': nrm((N_ODD, 2 * CONV_CH), 0.02),
        'w_dw': nrm((N_ODD, CONV_W, CONV_CH), CONV_W ** -0.5),
        'b_dw': nrm((N_ODD, CONV_CH), 0.02),
        'ln_g': gain((N_ODD, CONV_CH)),
        'ln_b': nrm((N_ODD, CONV_CH), 0.02),
        'w_pw2': nrm((N_ODD, CONV_CH, D_MODEL), CONV_CH ** -0.5),
        'b_pw2': nrm((N_ODD, D_MODEL), 0.02),
        'w_ffn_in': nrm((DEPTH, D_MODEL, 2 * FFN_DIM), D_MODEL ** -0.5),
        'w_ffn_dw': nrm((DEPTH, FFN_CONV_W, FFN_DIM), FFN_CONV_W ** -0.5),
        'b_ffn_dw': nrm((DEPTH, FFN_DIM), 0.02),
        'w_ffn_out': nrm((DEPTH, FFN_DIM, D_MODEL), FFN_DIM ** -0.5),
    }


def reference(x_prompt, x_sample, cache_fox_k, cache_fox_v, cache_fox_logf, page_table,
              state_hgrn, state_conv, state_ffn_conv, norm_mix, norm_ffn, norm_final,
              w_in0, fox_fb, hg_lb, hg_gnorm, w_out0, w_pw1, b_pw1, w_dw, b_dw, ln_g, ln_b,
              w_pw2, b_pw2, w_ffn_in, w_ffn_dw, b_ffn_dw, w_ffn_out):
    f32 = jnp.float32
    xp, xs = x_prompt, x_sample
    bp, bs = xp.shape[0], xs.shape[0]
    lb_all = jnp.cumsum(jax.nn.softmax(hg_lb.astype(f32), axis=0), axis=0)
    fk_p, fv_p, fl_p, fk_s, fv_s, fl_s = [], [], [], [], [], []
    hs_p, hs_s, cs_p, cs_s, ffs_p, ffs_s = [], [], [], [], [], []
    for l in range(DEPTH):
        if l % 2 == 0:
            e = l // 2
            hp = _rmsnorm(xp, norm_mix[l])
            out, fk, fv, flf, s_fin = _even_mix(hp, w_in0[e], fox_fb[e], lb_all[e], hg_gnorm[e], w_out0[e],
                                                jnp.zeros((bp, HG_HEADS, HG_DIM, HG_DIM), f32), None)
            xp = xp + out.astype(xp.dtype)
            fk_p.append(fk); fv_p.append(fv); fl_p.append(flf); hs_p.append(s_fin)
            kp = cache_fox_k[e][page_table].reshape(bs, -1, FOX_HEADS, FOX_DIM)
            vp = cache_fox_v[e][page_table].reshape(bs, -1, FOX_HEADS, FOX_DIM)
            lp = cache_fox_logf[e][page_table].reshape(bs, -1, FOX_HEADS)
            hsm = _rmsnorm(xs, norm_mix[l])
            out, fk, fv, flf, s_fin = _even_mix(hsm, w_in0[e], fox_fb[e], lb_all[e], hg_gnorm[e], w_out0[e],
                                                state_hgrn[e].astype(f32), (kp, vp, lp))
            xs = xs + out.astype(xs.dtype)
            fk_s.append(fk); fv_s.append(fv); fl_s.append(flf); hs_s.append(s_fin)
        else:
            o = l // 2
            hp = _rmsnorm(xp, norm_mix[l])
            out, hist = _conformer_conv(hp, jnp.zeros((bp, CONV_W - 1, CONV_CH), hp.dtype), w_pw1[o], b_pw1[o],
                                        w_dw[o], b_dw[o], ln_g[o], ln_b[o], w_pw2[o], b_pw2[o])
            xp = xp + out.astype(xp.dtype)
            cs_p.append(hist)
            hsm = _rmsnorm(xs, norm_mix[l])
            out, hist = _conformer_conv(hsm, state_conv[o], w_pw1[o], b_pw1[o], w_dw[o], b_dw[o],
                                        ln_g[o], ln_b[o], w_pw2[o], b_pw2[o])
            xs = xs + out.astype(xs.dtype)
            cs_s.append(hist)
        hp = _rmsnorm(xp, norm_ffn[l])
        out, hist = _conv_ffn(hp, jnp.zeros((bp, FFN_CONV_W - 1, FFN_DIM), hp.dtype),
                              w_ffn_in[l], w_ffn_dw[l], b_ffn_dw[l], w_ffn_out[l])
        xp = xp + out.astype(xp.dtype)
        ffs_p.append(hist)
        hsm = _rmsnorm(xs, norm_ffn[l])
        out, hist = _conv_ffn(hsm, state_ffn_conv[l], w_ffn_in[l], w_ffn_dw[l], b_ffn_dw[l], w_ffn_out[l])
        xs = xs + out.astype(xs.dtype)
        ffs_s.append(hist)
    y_prompt = _rmsnorm(xp, norm_final)
    y_sample = _rmsnorm(xs, norm_final)
    new_fox_k_prompt = jnp.stack(fk_p)
    new_fox_v_prompt = jnp.stack(fv_p)
    new_fox_logf_prompt = jnp.stack(fl_p)
    new_fox_k_sample = jnp.stack(fk_s)
    new_fox_v_sample = jnp.stack(fv_s)
    new_fox_logf_sample = jnp.stack(fl_s)
    new_hgrn_prompt = jnp.stack(hs_p)
    new_hgrn_sample = jnp.stack(hs_s)
    new_conv_prompt = jnp.stack(cs_p)
    new_conv_sample = jnp.stack(cs_s)
    new_ffn_conv_prompt = jnp.stack(ffs_p)
    new_ffn_conv_sample = jnp.stack(ffs_s)
    return (y_prompt, y_sample, new_fox_k_prompt, new_fox_v_prompt, new_fox_logf_prompt,
            new_fox_k_sample, new_fox_v_sample, new_fox_logf_sample, new_hgrn_prompt, new_hgrn_sample,
            new_conv_prompt, new_conv_sample, new_ffn_conv_prompt, new_ffn_conv_sample)
```

```python
import functools
import math

import numpy as np
import jax
import jax.numpy as jnp
from jax import lax
from jax.experimental import pallas as pl
from jax.experimental.pallas import tpu as pltpu

F32 = jnp.float32
BF16 = jnp.bfloat16
EPS = 1e-6
HG_DIM = 128
HG_CHUNK = 64
FOX_DIM = 64
LANES = 128
VMEM_LIMIT = 56 * 1024 * 1024
HIGHEST = lax.Precision.HIGHEST


def _cparams(sem):
    return pltpu.CompilerParams(dimension_semantics=sem, vmem_limit_bytes=VMEM_LIMIT)


def _tile(n, target):
    t = min(n, target)
    while n % t:
        t -= 8
    return t


def _rmsnorm(x, g):
    return x * lax.rsqrt(jnp.mean(x * x, axis=-1, keepdims=True) + EPS) * g


def _silu(x):
    return x * jax.nn.sigmoid(x)


def _log_sigmoid(x):
    return -(jnp.maximum(-x, 0.0) + jnp.log1p(jnp.exp(-jnp.abs(x))))


def _dot(a, b):
    return jnp.dot(a, b, preferred_element_type=F32)


def _dot_nt(a, b):
    return lax.dot_general(a, b, (((1,), (1,)), ((), ())), preferred_element_type=F32)


def _full(shape):
    return pl.BlockSpec(shape, lambda *_: (0,) * len(shape))


def _inproj_body(x_ref, g_ref, w_ref, wff_ref, fb_ref, lb_ref, tri_ref,
                 q_ref, fr_ref, v_ref, gate_ref, fq_ref, fk_ref, fv_ref, flf_ref, c_ref,
                 carry_ref, *, layer_e, tiles_per_seq, hgw):
    i = pl.program_id(0)
    h = _rmsnorm(x_ref[...], g_ref[...]).astype(BF16)

    def proj(k):
        return _dot(h, w_ref[:, k * hgw:(k + 1) * hgw])

    lbp = lb_ref[...]
    ex = jnp.exp(lbp - jnp.max(lbp, axis=0, keepdims=True))
    sm = ex / jnp.sum(ex, axis=0, keepdims=True)
    lb = jnp.sum(sm[:layer_e + 1], axis=0, keepdims=True)

    q_ref[...] = _silu(proj(0))
    fr_ref[...] = lb + (1.0 - lb) * jax.nn.sigmoid(proj(1))
    v_ref[...] = proj(2)
    gate_ref[...] = _silu(proj(3))
    fq_ref[...] = proj(4)
    fk_ref[...] = proj(5)
    fv_ref[...] = proj(6)
    flf = _log_sigmoid(_dot(h, wff_ref[...]) + fb_ref[...])
    flf_ref[...] = flf
    if tiles_per_seq is None:
        c_ref[...] = flf
    else:
        @pl.when(i % tiles_per_seq == 0)
        def _():
            carry_ref[...] = jnp.zeros_like(carry_ref)
        c = jnp.dot(tri_ref[...], flf, precision=HIGHEST, preferred_element_type=F32) + carry_ref[...]
        c_ref[...] = c
        carry_ref[...] = c[-1:, :]


def _inproj(x, g, w_main, w_ff, fb, hg_lb, *, layer_e, seq_len, tm):
    m, d = x.shape
    hgw = w_main.shape[1] // 7
    nfh = w_ff.shape[1]
    tiles_per_seq = None if seq_len == 1 else seq_len // tm
    tri = jnp.asarray(np.tril(np.ones((tm, tm), np.float32)))
    row = lambda n: pl.BlockSpec((tm, n), lambda i: (i, 0))
    wide = jax.ShapeDtypeStruct((m, hgw), F32)
    narrow = jax.ShapeDtypeStruct((m, nfh), F32)
    return pl.pallas_call(
        functools.partial(_inproj_body, layer_e=layer_e, tiles_per_seq=tiles_per_seq, hgw=hgw),
        grid=(m // tm,),
        in_specs=[row(d), _full((1, d)), _full(w_main.shape), _full(w_ff.shape), _full((1, nfh)),
                  _full(hg_lb.shape), _full((tm, tm))],
        out_specs=[row(hgw)] * 7 + [row(nfh)] * 2,
        out_shape=[wide] * 7 + [narrow] * 2,
        scratch_shapes=[pltpu.VMEM((1, nfh), F32)],
        compiler_params=_cparams(("arbitrary",)),
        name="inproj",
    )(x, g, w_main, w_ff, fb, hg_lb, tri)


def _hgrn_prompt_body(q_ref, fr_ref, v_ref, gate_ref, gn_ref, tri_ref, o_ref, s_ref, st_ref,
                      *, n_chunks, n_heads):
    t = pl.program_id(1)

    @pl.when(t == 0)
    def _():
        st_ref[...] = jnp.zeros_like(st_ref)

    tri = tri_ref[...]
    rows = lax.broadcasted_iota(jnp.int32, (HG_CHUNK, HG_CHUNK), 0)
    cols = lax.broadcasted_iota(jnp.int32, (HG_CHUNK, HG_CHUNK), 1)
    causal = rows >= cols
    gn = gn_ref[...]

    def chunk(c, carry):
        r0 = pl.multiple_of(c * HG_CHUNK, HG_CHUNK)
        rs = pl.ds(r0, HG_CHUNK)
        fr = fr_ref[rs, :]
        q = q_ref[rs, :]
        v = v_ref[rs, :]
        gate = gate_ref[rs, :]
        k = 1.0 - fr
        b = jnp.dot(tri, jnp.log(fr), precision=HIGHEST, preferred_element_type=F32)
        b_last = b[HG_CHUNK - 1:HG_CHUNK, :]
        b_mid = b[HG_CHUNK // 2:HG_CHUNK // 2 + 1, :]
        kdec = (k * jnp.exp(b_last - b)).astype(BF16)
        qb = (q * jnp.exp(b)).astype(BF16)
        qr = (q * jnp.exp(b - b_mid)).astype(BF16)
        kr = (k * jnp.exp(b_mid - b)).astype(BF16)
        dec = jnp.exp(b_last)
        for h in range(n_heads):
            sl = slice(h * HG_DIM, (h + 1) * HG_DIM)
            st = st_ref[h]
            o_inter = _dot_nt(qb[:, sl], st.astype(BF16))
            att = jnp.where(causal, _dot_nt(qr[:, sl], kr[:, sl]), 0.0)
            vh = v[:, sl]
            o = o_inter + _dot(att.astype(BF16), vh.astype(BF16))
            st_ref[h] = st * dec[:, sl] + _dot(vh.T.astype(BF16), kdec[:, sl])
            o = o * lax.rsqrt(jnp.mean(o * o, axis=-1, keepdims=True) + EPS)
            o_ref[rs, sl] = (o * gn[:, sl] * gate[:, sl]).astype(o_ref.dtype)
        return carry

    lax.fori_loop(0, n_chunks, chunk, 0)

    @pl.when(t == pl.num_programs(1) - 1)
    def _():
        for h in range(n_heads):
            s_ref[0, h] = st_ref[h].T


def _hgrn_prompt(q, fr, v, gate, gn, *, batch, seq_len, tc):
    m, hgw = q.shape
    n_heads = hgw // HG_DIM
    nt = seq_len // tc
    tri = jnp.asarray(np.tril(np.ones((HG_CHUNK, HG_CHUNK), np.float32)))
    row = pl.BlockSpec((tc, hgw), lambda b, t: (b * nt + t, 0))
    return pl.pallas_call(
        functools.partial(_hgrn_prompt_body, n_chunks=tc // HG_CHUNK, n_heads=n_heads),
        grid=(batch, nt),
        in_specs=[row, row, row, row, _full((1, hgw)), _full((HG_CHUNK, HG_CHUNK))],
        out_specs=[row, pl.BlockSpec((1, n_heads, HG_DIM, HG_DIM), lambda b, t: (b, 0, 0, 0))],
        out_shape=[jax.ShapeDtypeStruct((m, hgw), BF16),
                   jax.ShapeDtypeStruct((batch, n_heads, HG_DIM, HG_DIM), F32)],
        scratch_shapes=[pltpu.VMEM((n_heads, HG_DIM, HG_DIM), F32)],
        compiler_params=_cparams(("arbitrary", "arbitrary")),
        name="hgrn_prompt",
    )(q, fr, v, gate, gn, tri)


def _hgrn_sample_body(q_ref, fr_ref, v_ref, gate_ref, gn_ref, s_ref, o_ref, so_ref, *, bb, n_heads):
    def col(r):
        return jnp.broadcast_to(r, (HG_DIM, HG_DIM)).T

    for i in range(bb):
        for h in range(n_heads):
            sl = slice(h * HG_DIM, (h + 1) * HG_DIM)
            fr = fr_ref[i:i + 1, sl]
            dec = jnp.exp(jnp.log(fr))
            s_new = col(dec) * s_ref[i, h] + col(1.0 - fr) * v_ref[i:i + 1, sl]
            so_ref[i, h] = s_new
            o = jnp.sum(col(q_ref[i:i + 1, sl]) * s_new, axis=0, keepdims=True)
            o = o * lax.rsqrt(jnp.mean(o * o, axis=-1, keepdims=True) + EPS)
            o_ref[i:i + 1, sl] = (o * gn_ref[:, sl] * gate_ref[i:i + 1, sl]).astype(o_ref.dtype)


def _hgrn_sample(q, fr, v, gate, gn, state, *, bb):
    m, hgw = q.shape
    n_heads = hgw // HG_DIM
    row = pl.BlockSpec((bb, hgw), lambda i: (i, 0))
    st = pl.BlockSpec((bb, n_heads, HG_DIM, HG_DIM), lambda i: (i, 0, 0, 0))
    return pl.pallas_call(
        functools.partial(_hgrn_sample_body, bb=bb, n_heads=n_heads),
        grid=(m // bb,),
        in_specs=[row, row, row, row, _full((1, hgw)), st],
        out_specs=[row, st],
        out_shape=[jax.ShapeDtypeStruct((m, hgw), BF16), jax.ShapeDtypeStruct(state.shape, F32)],
        compiler_params=_cparams(("arbitrary",)),
        name="hgrn_sample",
    )(q, fr, v, gate, gn, state)


def _fox_prompt_body(qi_ref, ki_ref, q_ref, k_ref, v_ref, ccol_ref, crow_ref, o_ref,
                     m_ref, l_ref, acc_ref, *, tq):
    j = pl.program_id(2)
    qi = qi_ref[j]
    ki = ki_ref[j]
    scale = FOX_DIM ** -0.5

    @pl.when(ki == 0)
    def _():
        m_ref[...] = jnp.full_like(m_ref, -jnp.inf)
        l_ref[...] = jnp.zeros_like(l_ref)
        acc_ref[...] = jnp.zeros_like(acc_ref)

    lane = lax.broadcasted_iota(jnp.int32, (1, LANES), 1)

    def update(masked):
        q2 = q_ref[0] * scale
        k2 = k_ref[0].astype(BF16)
        v2 = v_ref[0].astype(BF16)
        if masked:
            rows = lax.broadcasted_iota(jnp.int32, (tq, tq), 0)
            cols = lax.broadcasted_iota(jnp.int32, (tq, tq), 1)
            keep = rows >= cols
        for hh in range(2):
            qm = jnp.where(lane // FOX_DIM == hh, q2, 0.0).astype(BF16)
            s = _dot_nt(qm, k2) - crow_ref[0, 0, hh:hh + 1, :]
            if masked:
                s = jnp.where(keep, s, -jnp.inf)
            ct = ccol_ref[0, 0, :, hh:hh + 1]
            m_prev = m_ref[hh]
            m_new = jnp.maximum(m_prev, jnp.max(s, axis=-1, keepdims=True) + ct)
            p = jnp.exp(s - (m_new - ct))
            alpha = jnp.exp(m_prev - m_new)
            l_ref[hh] = alpha * l_ref[hh] + jnp.sum(p, axis=-1, keepdims=True)
            acc_ref[hh] = alpha * acc_ref[hh] + _dot(p.astype(BF16), v2)
            m_ref[hh] = m_new

    @pl.when(ki < qi)
    def _():
        update(False)

    @pl.when(ki == qi)
    def _():
        update(True)
        o0 = acc_ref[0] / l_ref[0]
        o1 = acc_ref[1] / l_ref[1]
        o_ref[0] = jnp.where(lane // FOX_DIM == 0, o0, o1).astype(o_ref.dtype)


def _fox_prompt(fq, fk, fv, c, *, batch, seq_len, tq):
    m, fw = fq.shape
    n_heads = fw // FOX_DIM
    n_pairs = n_heads // 2
    nq = seq_len // tq
    qi_tab = np.array([q for q in range(nq) for _ in range(q + 1)], np.int32)
    ki_tab = np.array([k for q in range(nq) for k in range(q + 1)], np.int32)
    r3 = lambda a: a.reshape(batch, seq_len, fw)
    c4 = c.reshape(batch, seq_len, n_pairs, 2)
    ccol = c4.transpose(0, 2, 1, 3)
    crow = c4.transpose(0, 2, 3, 1)
    grid_spec = pltpu.PrefetchScalarGridSpec(
        num_scalar_prefetch=2,
        grid=(batch, n_pairs, len(qi_tab)),
        in_specs=[
            pl.BlockSpec((1, tq, LANES), lambda b, p, j, qt, kt: (b, qt[j], p)),
            pl.BlockSpec((1, tq, LANES), lambda b, p, j, qt, kt: (b, kt[j], p)),
            pl.BlockSpec((1, tq, LANES), lambda b, p, j, qt, kt: (b, kt[j], p)),
            pl.BlockSpec((1, 1, tq, 2), lambda b, p, j, qt, kt: (b, p, qt[j], 0)),
            pl.BlockSpec((1, 1, 2, tq), lambda b, p, j, qt, kt: (b, p, 0, kt[j])),
        ],
        out_specs=pl.BlockSpec((1, tq, LANES), lambda b, p, j, qt, kt: (b, qt[j], p)),
        scratch_shapes=[pltpu.VMEM((2, tq, 1), F32), pltpu.VMEM((2, tq, 1), F32),
                        pltpu.VMEM((2, tq, LANES), F32)],
    )
    o = pl.pallas_call(
        functools.partial(_fox_prompt_body, tq=tq),
        grid_spec=grid_spec,
        out_shape=jax.ShapeDtypeStruct((batch, seq_len, fw), BF16),
        compiler_params=_cparams(("arbitrary", "arbitrary", "arbitrary")),
        name="fox_prompt",
    )(jnp.asarray(qi_tab), jnp.asarray(ki_tab), r3(fq), r3(fk), r3(fv), ccol, crow)
    return o.reshape(m, fw)


def _fox_suffix_body(pt_ref, *refs, n_pages, n_heads):
    lf_refs, suf_ref = refs[:n_pages], refs[n_pages]
    width = lf_refs[0].shape[-1]
    lf = jnp.concatenate([r[0] for r in lf_refs], axis=0)
    lane = lax.broadcasted_iota(jnp.int32, lf.shape, 1)
    incl = lf
    tot = lf
    d = n_heads
    while d < width:
        shifted = pltpu.roll(incl, width - d, axis=1)
        incl = incl + jnp.where(lane + d < width, shifted, 0.0)
        tot = tot + pltpu.roll(tot, width - d, axis=1)
        d *= 2
    pr = lax.broadcasted_iota(jnp.int32, (n_pages, n_pages), 0)
    pc = lax.broadcasted_iota(jnp.int32, (n_pages, n_pages), 1)
    later = jnp.where(pc > pr, 1.0, 0.0).astype(F32)
    suf_ref[0] = (incl - lf) + jnp.dot(later, tot, precision=HIGHEST, preferred_element_type=F32)


def _fox_suffix(page_table, logf_flat, *, n_heads):
    bs, n_pages = page_table.shape
    n_pool, _, width = logf_flat.shape
    specs = [pl.BlockSpec((1, 1, width), functools.partial(lambda b, pt, g: (pt[b * n_pages + g], 0, 0), g=g))
             for g in range(n_pages)]
    grid_spec = pltpu.PrefetchScalarGridSpec(
        num_scalar_prefetch=1, grid=(bs,), in_specs=specs,
        out_specs=pl.BlockSpec((1, n_pages, width), lambda b, pt: (b, 0, 0)))
    return pl.pallas_call(
        functools.partial(_fox_suffix_body, n_pages=n_pages, n_heads=n_heads),
        grid_spec=grid_spec,
        out_shape=jax.ShapeDtypeStruct((bs, n_pages, width), F32),
        compiler_params=_cparams(("arbitrary",)),
        name="fox_suffix",
    )(page_table.reshape(-1), *([logf_flat] * n_pages))


def _fox_sample_body(pt_ref, *refs, group, n_heads):
    k_refs, v_refs = refs[:group], refs[group:2 * group]
    q_ref, kn_ref, vn_ref, cn_ref, suf_ref, o_ref, m_ref, l_ref, acc_ref = refs[2 * group:]
    j = pl.program_id(1)
    scale = FOX_DIM ** -0.5
    q = q_ref[0] * scale

    @pl.when(j == 0)
    def _():
        m_ref[...] = jnp.sum(q * kn_ref[0], axis=-1, keepdims=True)
        l_ref[...] = jnp.ones_like(l_ref)
        acc_ref[...] = vn_ref[0]

    qb = q.astype(BF16)
    cn = cn_ref[0]
    rows_per_page = k_refs[0].shape[2] * n_heads
    own = (lax.broadcasted_iota(jnp.int32, (n_heads, rows_per_page), 1) % n_heads
           == lax.broadcasted_iota(jnp.int32, (n_heads, rows_per_page), 0))
    for g in range(group):
        kp = k_refs[g][0, 0].reshape(rows_per_page, FOX_DIM).astype(BF16)
        vp = v_refs[g][0, 0].reshape(rows_per_page, FOX_DIM).astype(BF16)
        s = _dot_nt(qb, kp) + cn + suf_ref[0, g:g + 1, :]
        s = jnp.where(own, s, -jnp.inf)
        m_prev = m_ref[...]
        m_new = jnp.maximum(m_prev, jnp.max(s, axis=-1, keepdims=True))
        p = jnp.exp(s - m_new)
        alpha = jnp.exp(m_prev - m_new)
        l_ref[...] = alpha * l_ref[...] + jnp.sum(p, axis=-1, keepdims=True)
        acc_ref[...] = alpha * acc_ref[...] + _dot(p.astype(BF16), vp)
        m_ref[...] = m_new

    @pl.when(j == pl.num_programs(1) - 1)
    def _():
        o_ref[0] = acc_ref[...] / l_ref[...]


def _fox_sample(page_table, cache_k, cache_v, suffix, fq, fk, fv, flf, *, layer_e, group):
    bs, n_pages = page_table.shape
    _, n_pool, page, n_heads, _ = cache_k.shape
    steps = n_pages // group

    def page_spec(g):
        return pl.BlockSpec(
            (1, 1, page, n_heads, FOX_DIM),
            lambda b, j, pt: (layer_e, pt[b * n_pages + j * group + g], 0, 0, 0))

    tok = pl.BlockSpec((1, n_heads, FOX_DIM), lambda b, j, pt: (b, 0, 0))
    h3 = lambda a: a.reshape(bs, n_heads, FOX_DIM)
    grid_spec = pltpu.PrefetchScalarGridSpec(
        num_scalar_prefetch=1, grid=(bs, steps),
        in_specs=[page_spec(g) for g in range(group)] * 2 + [
            tok, tok, tok,
            pl.BlockSpec((1, n_heads, 1), lambda b, j, pt: (b, 0, 0)),
            pl.BlockSpec((1, group, page * n_heads), lambda b, j, pt: (b, j, 0))],
        out_specs=tok,
        scratch_shapes=[pltpu.VMEM((n_heads, 1), F32), pltpu.VMEM((n_heads, 1), F32),
                        pltpu.VMEM((n_heads, FOX_DIM), F32)])
    o = pl.pallas_call(
        functools.partial(_fox_sample_body, group=group, n_heads=n_heads),
        grid_spec=grid_spec,
        out_shape=jax.ShapeDtypeStruct((bs, n_heads, FOX_DIM), F32),
        compiler_params=_cparams(("arbitrary", "arbitrary")),
        name="fox_sample",
    )(page_table.reshape(-1), *([cache_k] * group), *([cache_v] * group),
      h3(fq), h3(fk), h3(fv), flf.reshape(bs, n_heads, 1), suffix)
    return o.reshape(bs, n_heads * FOX_DIM).astype(BF16)


def _outproj_body(x_ref, a_ref, b_ref, w_ref, o_ref):
    ka = a_ref.shape[1]
    o_ref[...] = x_ref[...] + _dot(a_ref[...], w_ref[:ka, :]) + _dot(b_ref[...], w_ref[ka:, :])


def _outproj(x, a, b, w, *, tm):
    m, d = x.shape
    row = lambda n: pl.BlockSpec((tm, n), lambda i: (i, 0))
    return pl.pallas_call(
        _outproj_body, grid=(m // tm,),
        in_specs=[row(d), row(a.shape[1]), row(b.shape[1]), _full(w.shape)],
        out_specs=row(d), out_shape=jax.ShapeDtypeStruct((m, d), F32),
        compiler_params=_cparams(("arbitrary",)), name="outproj",
    )(x, a, b, w)


def _gelu(x):
    return 0.5 * x * (1.0 + lax.erf(x * (2.0 ** -0.5)))


def _ffn_cols(ffn_dim):
    cw = 256
    while ffn_dim % cw:
        cw //= 2
    return cw


def _ffn_prompt_body(x_ref, g_ref, win_ref, wdw_ref, bdw_ref, wout_ref, gfin_ref,
                     o_ref, hist_ref, ext_ref, carry_ref, act_ref, *, tm, ffn_dim, cw, final_norm):
    t = pl.program_id(1)

    @pl.when(t == 0)
    def _():
        carry_ref[...] = jnp.zeros_like(carry_ref)

    x = x_ref[...]
    h = _rmsnorm(x, g_ref[...]).astype(BF16)
    for j in range(ffn_dim // cw):
        cs = slice(j * cw, (j + 1) * cw)
        g = _dot(h, win_ref[:, cs])
        u = _dot(h, win_ref[:, ffn_dim + j * cw:ffn_dim + (j + 1) * cw])
        ext_ref[6:8, :] = carry_ref[6:8, cs]
        ext_ref[8:, :] = g
        carry_ref[6:8, cs] = g[tm - 2:, :]
        gc = (wdw_ref[0:1, cs] * ext_ref[6:6 + tm, :] + wdw_ref[1:2, cs] * ext_ref[7:7 + tm, :]
              + wdw_ref[2:3, cs] * g + bdw_ref[:, cs])
        act_ref[:, cs] = (_gelu(gc) * u).astype(BF16)
    y = x + _dot(act_ref[...], wout_ref[...])
    if final_norm:
        y = _rmsnorm(y, gfin_ref[...])
    o_ref[...] = y

    @pl.when(t == pl.num_programs(1) - 1)
    def _():
        hist_ref[0] = carry_ref[6:8, :]


def _ffn_prompt(x, g, w_in, w_dw, b_dw, w_out, g_final, *, batch, seq_len, tm, final_norm):
    m, d = x.shape
    ffn_dim = w_out.shape[0]
    cw = _ffn_cols(ffn_dim)
    nt = seq_len // tm
    row = pl.BlockSpec((tm, d), lambda b, t: (b * nt + t, 0))
    return pl.pallas_call(
        functools.partial(_ffn_prompt_body, tm=tm, ffn_dim=ffn_dim, cw=cw, final_norm=final_norm),
        grid=(batch, nt),
        in_specs=[row, _full((1, d)), _full(w_in.shape), _full(w_dw.shape), _full((1, ffn_dim)),
                  _full(w_out.shape), _full((1, d))],
        out_specs=[row, pl.BlockSpec((1, 2, ffn_dim), lambda b, t: (b, 0, 0))],
        out_shape=[jax.ShapeDtypeStruct((m, d), F32), jax.ShapeDtypeStruct((batch, 2, ffn_dim), F32)],
        scratch_shapes=[pltpu.VMEM((tm + 8, cw), F32), pltpu.VMEM((8, ffn_dim), F32),
                        pltpu.VMEM((tm, ffn_dim), BF16)],
        compiler_params=_cparams(("arbitrary", "arbitrary")),
        name="ffn_prompt",
    )(x, g, w_in, w_dw, b_dw, w_out, g_final)


def _ffn_sample_body(x_ref, g_ref, win_ref, wdw_ref, bdw_ref, wout_ref, gfin_ref, hist_ref,
                     o_ref, nh_ref, act_ref, *, ffn_dim, cw, final_norm):
    x = x_ref[...]
    h = _rmsnorm(x, g_ref[...]).astype(BF16)
    for j in range(ffn_dim // cw):
        cs = slice(j * cw, (j + 1) * cw)
        g = _dot(h, win_ref[:, cs])
        u = _dot(h, win_ref[:, ffn_dim + j * cw:ffn_dim + (j + 1) * cw])
        h0 = hist_ref[:, cs]
        h1 = hist_ref[:, ffn_dim + j * cw:ffn_dim + (j + 1) * cw]
        gc = wdw_ref[0:1, cs] * h0 + wdw_ref[1:2, cs] * h1 + wdw_ref[2:3, cs] * g + bdw_ref[:, cs]
        act_ref[:, cs] = (_gelu(gc) * u).astype(BF16)
        nh_ref[:, cs] = h1
        nh_ref[:, ffn_dim + j * cw:ffn_dim + (j + 1) * cw] = g
    y = x + _dot(act_ref[...], wout_ref[...])
    if final_norm:
        y = _rmsnorm(y, gfin_ref[...])
    o_ref[...] = y


def _ffn_sample(x, g, w_in, w_dw, b_dw, w_out, g_final, hist, *, final_norm):
    m, d = x.shape
    ffn_dim = w_out.shape[0]
    cw = _ffn_cols(ffn_dim)
    hist2 = hist.reshape(m, 2 * ffn_dim)
    y, nh = pl.pallas_call(
        functools.partial(_ffn_sample_body, ffn_dim=ffn_dim, cw=cw, final_norm=final_norm),
        grid=(1,),
        in_specs=[_full((m, d)), _full((1, d)), _full(w_in.shape), _full(w_dw.shape),
                  _full((1, ffn_dim)), _full(w_out.shape), _full((1, d)), _full(hist2.shape)],
        out_specs=[_full((m, d)), _full(hist2.shape)],
        out_shape=[jax.ShapeDtypeStruct((m, d), F32), jax.ShapeDtypeStruct(hist2.shape, F32)],
        scratch_shapes=[pltpu.VMEM((m, ffn_dim), BF16)],
        compiler_params=_cparams(("arbitrary",)),
        name="ffn_sample",
    )(x, g, w_in, w_dw, b_dw, w_out, g_final, hist2)
    return y, nh.reshape(m, 2, ffn_dim)


def _layernorm_silu(d, g, b):
    dc = d - jnp.mean(d, axis=-1, keepdims=True)
    y = dc * lax.rsqrt(jnp.mean(dc * dc, axis=-1, keepdims=True) + EPS) * g + b
    return _silu(y)


def _conf_prompt_body(x_ref, g_ref, w1_ref, b1_ref, wdw_ref, bdw_ref, lng_ref, lnb_ref, w2_ref, b2_ref,
                      o_ref, hist_ref, ext_ref, *, tm, ch, cw, pad):
    t = pl.program_id(1)

    @pl.when(t == 0)
    def _():
        ext_ref[0:pad, :] = jnp.zeros((pad, ch), F32)

    x = x_ref[...]
    h = _rmsnorm(x, g_ref[...]).astype(BF16)
    a = _dot(h, w1_ref[:, :ch]) + b1_ref[:, :ch]
    gt = _dot(h, w1_ref[:, ch:]) + b1_ref[:, ch:]
    ext_ref[pad:, :] = a * jax.nn.sigmoid(gt)
    off = pad - (cw - 1)
    d = bdw_ref[...] + wdw_ref[0:1, :] * ext_ref[off:off + tm, :]
    for j in range(1, cw):
        d = d + wdw_ref[j:j + 1, :] * ext_ref[off + j:off + j + tm, :]
    y = _layernorm_silu(d, lng_ref[...], lnb_ref[...]).astype(BF16)
    o_ref[...] = x + _dot(y, w2_ref[...]) + b2_ref[...]

    @pl.when(t == pl.num_programs(1) - 1)
    def _():
        hist_ref[0] = ext_ref[tm + off:tm + pad, :]

    ext_ref[0:pad, :] = ext_ref[tm:tm + pad, :]


def _conf_prompt(x, g, w1, b1, wdw, bdw, lng, lnb, w2, b2, *, batch, seq_len, tm):
    m, d = x.shape
    cw, ch = wdw.shape
    pad = -(-(cw - 1) // 8) * 8
    nt = seq_len // tm
    row = pl.BlockSpec((tm, d), lambda b, t: (b * nt + t, 0))
    return pl.pallas_call(
        functools.partial(_conf_prompt_body, tm=tm, ch=ch, cw=cw, pad=pad),
        grid=(batch, nt),
        in_specs=[row, _full((1, d)), _full(w1.shape), _full((1, 2 * ch)), _full(wdw.shape), _full((1, ch)),
                  _full((1, ch)), _full((1, ch)), _full(w2.shape), _full((1, d))],
        out_specs=[row, pl.BlockSpec((1, cw - 1, ch), lambda b, t: (b, 0, 0))],
        out_shape=[jax.ShapeDtypeStruct((m, d), F32), jax.ShapeDtypeStruct((batch, cw - 1, ch), F32)],
        scratch_shapes=[pltpu.VMEM((tm + pad, ch), F32)],
        compiler_params=_cparams(("arbitrary", "arbitrary")),
        name="conf_prompt",
    )(x, g, w1, b1, wdw, bdw, lng, lnb, w2, b2)


def _conf_sample_body(x_ref, g_ref, w1_ref, b1_ref, wdw_ref, bdw_ref, lng_ref, lnb_ref, w2_ref, b2_ref,
                      hist_ref, o_ref, u_ref, *, ch, cw):
    x = x_ref[...]
    h = _rmsnorm(x, g_ref[...]).astype(BF16)
    a = _dot(h, w1_ref[:, :ch]) + b1_ref[:, :ch]
    gt = _dot(h, w1_ref[:, ch:]) + b1_ref[:, ch:]
    u = a * jax.nn.sigmoid(gt)
    u_ref[...] = u
    d = jnp.sum(hist_ref[...] * wdw_ref[0:cw - 1, :][None], axis=1)
    d = d + wdw_ref[cw - 1:cw, :] * u + bdw_ref[...]
    y = _layernorm_silu(d, lng_ref[...], lnb_ref[...]).astype(BF16)
    o_ref[...] = x + _dot(y, w2_ref[...]) + b2_ref[...]


def _conf_sample(x, g, w1, b1, wdw, bdw, lng, lnb, w2, b2, hist, *, bb):
    m, d = x.shape
    cw, ch = wdw.shape
    row = lambda n: pl.BlockSpec((bb, n), lambda i: (i, 0))
    return pl.pallas_call(
        functools.partial(_conf_sample_body, ch=ch, cw=cw),
        grid=(m // bb,),
        in_specs=[row(d), _full((1, d)), _full(w1.shape), _full((1, 2 * ch)), _full(wdw.shape), _full((1, ch)),
                  _full((1, ch)), _full((1, ch)), _full(w2.shape), _full((1, d)),
                  pl.BlockSpec((bb, cw - 1, ch), lambda i: (i, 0, 0))],
        out_specs=[row(d), row(ch)],
        out_shape=[jax.ShapeDtypeStruct((m, d), F32), jax.ShapeDtypeStruct((m, ch), F32)],
        compiler_params=_cparams(("arbitrary",)),
        name="conf_sample",
    )(x, g, w1, b1, wdw, bdw, lng, lnb, w2, b2, hist)


def kernel(x_prompt, x_sample, cache_fox_k, cache_fox_v, cache_fox_logf, page_table, state_hgrn, state_conv,
           state_ffn_conv, norm_mix, norm_ffn, norm_final, w_in0, fox_fb, hg_lb, hg_gnorm, w_out0, w_pw1, b_pw1,
           w_dw, b_dw, ln_g, ln_b, w_pw2, b_pw2, w_ffn_in, w_ffn_dw, b_ffn_dw, w_ffn_out):
    bp, seq_len, d = x_prompt.shape
    bs, dec_seq, _ = x_sample.shape
    assert dec_seq == 1, "the sample path handles one new token per sequence"
    depth = norm_mix.shape[0]
    n_fox_heads = fox_fb.shape[1]
    fox_w = n_fox_heads * FOX_DIM
    hg_w = hg_gnorm.shape[1]
    hg_heads = hg_w // HG_DIM
    page = cache_fox_k.shape[2]
    row2 = lambda a: a.reshape(1, -1)

    xp = x_prompt.reshape(bp * seq_len, d)
    xs = x_sample.reshape(bs, d)
    tm = _tile(seq_len, 512)

    fk_p, fv_p, fl_p, fk_s, fv_s, fl_s = [], [], [], [], [], []
    hs_p, hs_s, cs_p, cs_s, ffs_p, ffs_s = [], [], [], [], [], []
    for l in range(depth):
        if l % 2 == 0:
            e = l // 2
            w_main = w_in0[e][:, :7 * hg_w].astype(BF16)
            w_ff = w_in0[e][:, 7 * hg_w:].astype(BF16)
            w_out = w_out0[e].astype(BF16)
            gn = row2(hg_gnorm[e])
            fb = row2(fox_fb[e])
            q, fr, v, gate, fq, fk, fv, flf, c = _inproj(
                xp, row2(norm_mix[l]), w_main, w_ff, fb, hg_lb, layer_e=e, seq_len=seq_len, tm=tm)
            o_hg, s_fin = _hgrn_prompt(q, fr, v, gate, gn, batch=bp, seq_len=seq_len, tc=tm)
            o_fox = _fox_prompt(fq, fk, fv, c, batch=bp, seq_len=seq_len, tq=tm)
            xp = _outproj(xp, o_hg, o_fox, w_out, tm=tm)
            fk_p.append(fk.reshape(bp, seq_len, n_fox_heads, FOX_DIM))
            fv_p.append(fv.reshape(bp, seq_len, n_fox_heads, FOX_DIM))
            fl_p.append(flf.reshape(bp, seq_len, n_fox_heads))
            hs_p.append(s_fin)
            q, fr, v, gate, fq, fk, fv, flf, _ = _inproj(
                xs, row2(norm_mix[l]), w_main, w_ff, fb, hg_lb, layer_e=e, seq_len=1, tm=bs)
            o_hg, s_fin = _hgrn_sample(q, fr, v, gate, gn, state_hgrn[e], bb=8)
            logf_flat = cache_fox_logf[e].reshape(-1, 1, page * n_fox_heads)
            suffix = _fox_suffix(page_table, logf_flat, n_heads=n_fox_heads)
            o_fox = _fox_sample(page_table, cache_fox_k, cache_fox_v, suffix, fq, fk, fv, flf,
                                layer_e=e, group=8)
            xs = _outproj(xs, o_hg, o_fox, w_out, tm=bs)
            fk_s.append(fk.reshape(bs, 1, n_fox_heads, FOX_DIM))
            fv_s.append(fv.reshape(bs, 1, n_fox_heads, FOX_DIM))
            fl_s.append(flf.reshape(bs, 1, n_fox_heads))
            hs_s.append(s_fin)
        else:
            o = l // 2
            args = (row2(norm_mix[l]), w_pw1[o].astype(BF16), row2(b_pw1[o]), w_dw[o], row2(b_dw[o]),
                    row2(ln_g[o]), row2(ln_b[o]), w_pw2[o].astype(BF16), row2(b_pw2[o]))
            xp, hist = _conf_prompt(xp, *args, batch=bp, seq_len=seq_len, tm=_tile(seq_len, 256))
            cs_p.append(hist)
            xs, u = _conf_sample(xs, *args, state_conv[o], bb=32)
            cs_s.append(jnp.concatenate([state_conv[o][:, 1:], u[:, None, :]], axis=1))
        last = l == depth - 1
        fargs = (row2(norm_ffn[l]), w_ffn_in[l].astype(BF16), w_ffn_dw[l], row2(b_ffn_dw[l]),
                 w_ffn_out[l].astype(BF16), row2(norm_final))
        xp, hist = _ffn_prompt(xp, *fargs, batch=bp, seq_len=seq_len, tm=tm, final_norm=last)
        ffs_p.append(hist)
        xs, hist = _ffn_sample(xs, *fargs, state_ffn_conv[l], final_norm=last)
        ffs_s.append(hist)

    return (xp.reshape(bp, seq_len, d), xs.reshape(bs, 1, d),
            jnp.stack(fk_p), jnp.stack(fv_p), jnp.stack(fl_p),
            jnp.stack(fk_s), jnp.stack(fv_s), jnp.stack(fl_s),
            jnp.stack(hs_p), jnp.stack(hs_s), jnp.stack(cs_p), jnp.stack(cs_s),
            jnp.stack(ffs_p), jnp.stack(ffs_s))
```

```python
import functools
import math

import numpy as np
import jax
import jax.numpy as jnp
from jax import lax
from jax.experimental import pallas as pl
from jax.experimental.pallas import tpu as pltpu

F32 = jnp.float32
BF16 = jnp.bfloat16
EPS = 1e-6
HG_DIM = 128
HG_CHUNK = 64
FOX_DIM = 64
LANES = 128
VMEM_LIMIT = 56 * 1024 * 1024
HIGHEST = lax.Precision.HIGHEST


def _cparams(sem):
    return pltpu.CompilerParams(dimension_semantics=sem, vmem_limit_bytes=VMEM_LIMIT)


def _tile(n, target):
    t = min(n, target)
    while n % t:
        t -= 8
    return t


def _rmsnorm(x, g):
    return x * lax.rsqrt(jnp.mean(x * x, axis=-1, keepdims=True) + EPS) * g


def _silu(x):
    return x * jax.nn.sigmoid(x)


def _log_sigmoid(x):
    return -(jnp.maximum(-x, 0.0) + jnp.log1p(jnp.exp(-jnp.abs(x))))


def _dot(a, b):
    return jnp.dot(a, b, preferred_element_type=F32)


def _dot_nt(a, b):
    return lax.dot_general(a, b, (((1,), (1,)), ((), ())), preferred_element_type=F32)


def _full(shape):
    return pl.BlockSpec(shape, lambda *_: (0,) * len(shape))


def _inproj_common(x_ref, g_ref, w_ref, lb_ref, q_ref, fr_ref, v_ref, gate_ref, fq_ref, *, layer_e, hgw):
    h = _rmsnorm(x_ref[...], g_ref[...]).astype(BF16)

    def proj(k):
        return _dot(h, w_ref[:, k * hgw:(k + 1) * hgw])

    lbp = lb_ref[...]
    ex = jnp.exp(lbp - jnp.max(lbp, axis=0, keepdims=True))
    sm = ex / jnp.sum(ex, axis=0, keepdims=True)
    lb = jnp.sum(sm[:layer_e + 1], axis=0, keepdims=True)

    q_ref[...] = _silu(proj(0))
    fr_ref[...] = lb + (1.0 - lb) * jax.nn.sigmoid(proj(1))
    v_ref[...] = proj(2)
    gate_ref[...] = _silu(proj(3))
    fq_ref[...] = proj(4)
    return h, proj


def _inproj_prompt_body(x_ref, g_ref, w_ref, wkvt_ref, wff_ref, wfft_ref, fb_ref, fbt_ref, lb_ref, tri_ref,
                        trit_ref, q_ref, fr_ref, v_ref, gate_ref, fq_ref, fkt_ref, fvt_ref, flft_ref, c_ref,
                        ct_ref, carry_ref, carryt_ref, *, layer_e, hgw):
    h, proj = _inproj_common(x_ref, g_ref, w_ref, lb_ref, q_ref, fr_ref, v_ref, gate_ref, fq_ref,
                             layer_e=layer_e, hgw=hgw)
    fkt_ref[0] = _dot_nt(wkvt_ref[:hgw, :], h)
    fvt_ref[0] = _dot_nt(wkvt_ref[hgw:, :], h)

    @pl.when(pl.program_id(1) == 0)
    def _():
        carry_ref[...] = jnp.zeros_like(carry_ref)
        carryt_ref[...] = jnp.zeros_like(carryt_ref)

    flf = _log_sigmoid(_dot(h, wff_ref[...]) + fb_ref[...])
    c = jnp.dot(tri_ref[...], flf, precision=HIGHEST, preferred_element_type=F32) + carry_ref[...]
    c_ref[...] = c
    carry_ref[...] = c[-1:, :]
    flft = _log_sigmoid(_dot_nt(wfft_ref[...], h) + fbt_ref[...])
    flft_ref[0] = flft
    ct = jnp.dot(flft, trit_ref[...], precision=HIGHEST, preferred_element_type=F32) + carryt_ref[...]
    ct_ref[0] = ct
    carryt_ref[...] = ct[:, -1:]


def _inproj_prompt(x, g, w_main, w_kvt, w_ff, fb, hg_lb, *, layer_e, batch, seq_len, tm):
    m, d = x.shape
    hgw = w_kvt.shape[0] // 2
    nfh = w_ff.shape[1]
    nt = seq_len // tm
    tri = np.tril(np.ones((tm, tm), np.float32))
    row = lambda n: pl.BlockSpec((tm, n), lambda b, t: (b * nt + t, 0))
    tr = lambda n: pl.BlockSpec((1, n, tm), lambda b, t: (b, 0, t))
    wide = jax.ShapeDtypeStruct((m, hgw), F32)
    wide_t = jax.ShapeDtypeStruct((batch, hgw, seq_len), F32)
    narrow_t = jax.ShapeDtypeStruct((batch, nfh, seq_len), F32)
    return pl.pallas_call(
        functools.partial(_inproj_prompt_body, layer_e=layer_e, hgw=hgw),
        grid=(batch, nt),
        in_specs=[row(d), _full((1, d)), _full(w_main.shape), _full(w_kvt.shape), _full(w_ff.shape),
                  _full((nfh, d)), _full((1, nfh)), _full((nfh, 1)), _full(hg_lb.shape), _full((tm, tm)),
                  _full((tm, tm))],
        out_specs=[row(hgw)] * 5 + [tr(hgw), tr(hgw), tr(nfh), row(nfh), tr(nfh)],
        out_shape=[wide] * 5 + [wide_t, wide_t, narrow_t, jax.ShapeDtypeStruct((m, nfh), F32), narrow_t],
        scratch_shapes=[pltpu.VMEM((1, nfh), F32), pltpu.VMEM((nfh, 1), F32)],
        compiler_params=_cparams(("arbitrary", "arbitrary")),
        name="inproj_prompt",
    )(x, g, w_main, w_kvt, w_ff, w_ff.T, fb, fb.reshape(nfh, 1), hg_lb, jnp.asarray(tri), jnp.asarray(tri.T))


def _inproj_sample_body(x_ref, g_ref, w_ref, wff_ref, fb_ref, lb_ref,
                        q_ref, fr_ref, v_ref, gate_ref, fq_ref, fk_ref, fv_ref, flf_ref, *, layer_e, hgw):
    h, proj = _inproj_common(x_ref, g_ref, w_ref, lb_ref, q_ref, fr_ref, v_ref, gate_ref, fq_ref,
                             layer_e=layer_e, hgw=hgw)
    fk_ref[...] = proj(5)
    fv_ref[...] = proj(6)
    flf_ref[...] = _log_sigmoid(_dot(h, wff_ref[...]) + fb_ref[...])


def _inproj_sample(x, g, w_main, w_ff, fb, hg_lb, *, layer_e):
    m, d = x.shape
    hgw = w_main.shape[1] // 7
    nfh = w_ff.shape[1]
    wide = jax.ShapeDtypeStruct((m, hgw), F32)
    return pl.pallas_call(
        functools.partial(_inproj_sample_body, layer_e=layer_e, hgw=hgw),
        grid=(1,),
        in_specs=[_full((m, d)), _full((1, d)), _full(w_main.shape), _full(w_ff.shape), _full((1, nfh)),
                  _full(hg_lb.shape)],
        out_specs=[_full((m, hgw))] * 7 + [_full((m, nfh))],
        out_shape=[wide] * 7 + [jax.ShapeDtypeStruct((m, nfh), F32)],
        compiler_params=_cparams(("arbitrary",)),
        name="inproj_sample",
    )(x, g, w_main, w_ff, fb, hg_lb)


def _hgrn_prompt_body(q_ref, fr_ref, v_ref, gate_ref, gn_ref, tri_ref, o_ref, s_ref, st_ref,
                      *, n_chunks, n_heads):
    t = pl.program_id(1)

    @pl.when(t == 0)
    def _():
        st_ref[...] = jnp.zeros_like(st_ref)

    tri = tri_ref[...]
    rows = lax.broadcasted_iota(jnp.int32, (HG_CHUNK, HG_CHUNK), 0)
    cols = lax.broadcasted_iota(jnp.int32, (HG_CHUNK, HG_CHUNK), 1)
    causal = rows >= cols
    gn = gn_ref[...]

    def chunk(c, carry):
        r0 = pl.multiple_of(c * HG_CHUNK, HG_CHUNK)
        rs = pl.ds(r0, HG_CHUNK)
        fr = fr_ref[rs, :]
        q = q_ref[rs, :]
        v = v_ref[rs, :]
        gate = gate_ref[rs, :]
        k = 1.0 - fr
        b = jnp.dot(tri, jnp.log(fr), precision=HIGHEST, preferred_element_type=F32)
        b_last = b[HG_CHUNK - 1:HG_CHUNK, :]
        b_mid = b[HG_CHUNK // 2:HG_CHUNK // 2 + 1, :]
        kdec = (k * jnp.exp(b_last - b)).astype(BF16)
        qb = (q * jnp.exp(b)).astype(BF16)
        qr = (q * jnp.exp(b - b_mid)).astype(BF16)
        kr = (k * jnp.exp(b_mid - b)).astype(BF16)
        dec = jnp.exp(b_last)
        for h in range(n_heads):
            sl = slice(h * HG_DIM, (h + 1) * HG_DIM)
            st = st_ref[h]
            o_inter = _dot_nt(qb[:, sl], st.astype(BF16))
            att = jnp.where(causal, _dot_nt(qr[:, sl], kr[:, sl]), 0.0)
            vh = v[:, sl]
            o = o_inter + _dot(att.astype(BF16), vh.astype(BF16))
            st_ref[h] = st * dec[:, sl] + _dot(vh.T.astype(BF16), kdec[:, sl])
            o = o * lax.rsqrt(jnp.mean(o * o, axis=-1, keepdims=True) + EPS)
            o_ref[rs, sl] = (o * gn[:, sl] * gate[:, sl]).astype(o_ref.dtype)
        return carry

    lax.fori_loop(0, n_chunks, chunk, 0)

    @pl.when(t == pl.num_programs(1) - 1)
    def _():
        for h in range(n_heads):
            s_ref[0, h] = st_ref[h].T


def _hgrn_prompt(q, fr, v, gate, gn, *, batch, seq_len, tc):
    m, hgw = q.shape
    n_heads = hgw // HG_DIM
    nt = seq_len // tc
    tri = jnp.asarray(np.tril(np.ones((HG_CHUNK, HG_CHUNK), np.float32)))
    row = pl.BlockSpec((tc, hgw), lambda b, t: (b * nt + t, 0))
    return pl.pallas_call(
        functools.partial(_hgrn_prompt_body, n_chunks=tc // HG_CHUNK, n_heads=n_heads),
        grid=(batch, nt),
        in_specs=[row, row, row, row, _full((1, hgw)), _full((HG_CHUNK, HG_CHUNK))],
        out_specs=[row, pl.BlockSpec((1, n_heads, HG_DIM, HG_DIM), lambda b, t: (b, 0, 0, 0))],
        out_shape=[jax.ShapeDtypeStruct((m, hgw), BF16),
                   jax.ShapeDtypeStruct((batch, n_heads, HG_DIM, HG_DIM), F32)],
        scratch_shapes=[pltpu.VMEM((n_heads, HG_DIM, HG_DIM), F32)],
        compiler_params=_cparams(("arbitrary", "arbitrary")),
        name="hgrn_prompt",
    )(q, fr, v, gate, gn, tri)


def _hgrn_sample_body(q_ref, fr_ref, v_ref, gate_ref, gn_ref, s_ref, o_ref, so_ref, *, bb, n_heads):
    def col(r):
        return jnp.broadcast_to(r, (HG_DIM, HG_DIM)).T

    for i in range(bb):
        for h in range(n_heads):
            sl = slice(h * HG_DIM, (h + 1) * HG_DIM)
            fr = fr_ref[i:i + 1, sl]
            dec = jnp.exp(jnp.log(fr))
            s_new = col(dec) * s_ref[i, h] + col(1.0 - fr) * v_ref[i:i + 1, sl]
            so_ref[i, h] = s_new
            o = jnp.sum(col(q_ref[i:i + 1, sl]) * s_new, axis=0, keepdims=True)
            o = o * lax.rsqrt(jnp.mean(o * o, axis=-1, keepdims=True) + EPS)
            o_ref[i:i + 1, sl] = (o * gn_ref[:, sl] * gate_ref[i:i + 1, sl]).astype(o_ref.dtype)


def _hgrn_sample(q, fr, v, gate, gn, state, *, bb):
    m, hgw = q.shape
    n_heads = hgw // HG_DIM
    row = pl.BlockSpec((bb, hgw), lambda i: (i, 0))
    st = pl.BlockSpec((bb, n_heads, HG_DIM, HG_DIM), lambda i: (i, 0, 0, 0))
    return pl.pallas_call(
        functools.partial(_hgrn_sample_body, bb=bb, n_heads=n_heads),
        grid=(m // bb,),
        in_specs=[row, row, row, row, _full((1, hgw)), st],
        out_specs=[row, st],
        out_shape=[jax.ShapeDtypeStruct((m, hgw), BF16), jax.ShapeDtypeStruct(state.shape, F32)],
        compiler_params=_cparams(("arbitrary",)),
        name="hgrn_sample",
    )(q, fr, v, gate, gn, state)


def _fox_prompt_body(qi_ref, ki_ref, q_ref, k_ref, v_ref, ccol_ref, crow_ref, o_ref,
                     m_ref, l_ref, acc_ref, *, tq):
    j = pl.program_id(2)
    qi = qi_ref[j]
    ki = ki_ref[j]
    scale = FOX_DIM ** -0.5

    @pl.when(ki == 0)
    def _():
        m_ref[...] = jnp.full_like(m_ref, -jnp.inf)
        l_ref[...] = jnp.zeros_like(l_ref)
        acc_ref[...] = jnp.zeros_like(acc_ref)

    lane = lax.broadcasted_iota(jnp.int32, (1, LANES), 1)

    def update(masked):
        q2 = q_ref[0] * scale
        k2 = k_ref[0].astype(BF16)
        v2 = v_ref[0].astype(BF16)
        if masked:
            rows = lax.broadcasted_iota(jnp.int32, (tq, tq), 0)
            cols = lax.broadcasted_iota(jnp.int32, (tq, tq), 1)
            keep = rows >= cols
        for hh in range(2):
            qm = jnp.where(lane // FOX_DIM == hh, q2, 0.0).astype(BF16)
            s = _dot(qm, k2) - crow_ref[0, 0, hh:hh + 1, :]
            if masked:
                s = jnp.where(keep, s, -jnp.inf)
            ct = ccol_ref[0, 0, :, hh:hh + 1]
            m_prev = m_ref[hh]
            m_new = jnp.maximum(m_prev, jnp.max(s, axis=-1, keepdims=True) + ct)
            p = jnp.exp(s - (m_new - ct))
            alpha = jnp.exp(m_prev - m_new)
            l_ref[hh] = alpha * l_ref[hh] + jnp.sum(p, axis=-1, keepdims=True)
            acc_ref[hh] = alpha * acc_ref[hh] + _dot_nt(p.astype(BF16), v2)
            m_ref[hh] = m_new

    @pl.when(ki < qi)
    def _():
        update(False)

    @pl.when(ki == qi)
    def _():
        update(True)
        o0 = acc_ref[0] / l_ref[0]
        o1 = acc_ref[1] / l_ref[1]
        o_ref[0] = jnp.where(lane // FOX_DIM == 0, o0, o1).astype(o_ref.dtype)


def _fox_prompt(fq, fkt, fvt, c, ct, *, batch, seq_len, tq):
    m, fw = fq.shape
    n_heads = fw // FOX_DIM
    n_pairs = n_heads // 2
    nq = seq_len // tq
    qi_tab = np.array([q for q in range(nq) for _ in range(q + 1)], np.int32)
    ki_tab = np.array([k for q in range(nq) for k in range(q + 1)], np.int32)
    r3 = lambda a: a.reshape(batch, seq_len, fw)
    ccol = c.reshape(batch, seq_len, n_pairs, 2).transpose(0, 2, 1, 3)
    crow = ct.reshape(batch, n_pairs, 2, seq_len)
    grid_spec = pltpu.PrefetchScalarGridSpec(
        num_scalar_prefetch=2,
        grid=(batch, n_pairs, len(qi_tab)),
        in_specs=[
            pl.BlockSpec((1, tq, LANES), lambda b, p, j, qt, kt: (b, qt[j], p)),
            pl.BlockSpec((1, LANES, tq), lambda b, p, j, qt, kt: (b, p, kt[j])),
            pl.BlockSpec((1, LANES, tq), lambda b, p, j, qt, kt: (b, p, kt[j])),
            pl.BlockSpec((1, 1, tq, 2), lambda b, p, j, qt, kt: (b, p, qt[j], 0)),
            pl.BlockSpec((1, 1, 2, tq), lambda b, p, j, qt, kt: (b, p, 0, kt[j])),
        ],
        out_specs=pl.BlockSpec((1, tq, LANES), lambda b, p, j, qt, kt: (b, qt[j], p)),
        scratch_shapes=[pltpu.VMEM((2, tq, 1), F32), pltpu.VMEM((2, tq, 1), F32),
                        pltpu.VMEM((2, tq, LANES), F32)],
    )
    o = pl.pallas_call(
        functools.partial(_fox_prompt_body, tq=tq),
        grid_spec=grid_spec,
        out_shape=jax.ShapeDtypeStruct((batch, seq_len, fw), BF16),
        compiler_params=_cparams(("arbitrary", "arbitrary", "arbitrary")),
        name="fox_prompt",
    )(jnp.asarray(qi_tab), jnp.asarray(ki_tab), r3(fq), fkt, fvt, ccol, crow)
    return o.reshape(m, fw)


def _fox_sample_body(pt_ref, *refs, n_pages, n_heads):
    k_refs, v_refs, lf_refs = refs[:n_pages], refs[n_pages:2 * n_pages], refs[2 * n_pages:3 * n_pages]
    q_ref, kn_ref, vn_ref, cn_ref, o_ref = refs[3 * n_pages:]
    page = lf_refs[0].shape[-1]
    width = n_heads * FOX_DIM
    rows = n_pages * n_heads
    scale = FOX_DIM ** -0.5

    diag = (lax.broadcasted_iota(jnp.int32, (n_heads, width), 1) // FOX_DIM
            == lax.broadcasted_iota(jnp.int32, (n_heads, width), 0))
    qblk = jnp.where(diag, q_ref[0] * scale, 0.0)
    s_new = jnp.sum(qblk * kn_ref[0], axis=-1, keepdims=True)

    lf = jnp.concatenate([r[0, 0] for r in lf_refs], axis=0)
    pj = lax.broadcasted_iota(jnp.int32, (page, page), 0)
    ps = lax.broadcasted_iota(jnp.int32, (page, page), 1)
    within = jnp.dot(lf, jnp.where(pj > ps, 1.0, 0.0).astype(F32), precision=HIGHEST,
                     preferred_element_type=F32)
    rr = lax.broadcasted_iota(jnp.int32, (rows, rows), 0)
    rc = lax.broadcasted_iota(jnp.int32, (rows, rows), 1)
    later_pages = jnp.where(rc % n_heads == rr % n_heads, jnp.where(rc // n_heads > rr // n_heads, 1.0, 0.0), 0.0)
    tot = jnp.broadcast_to(jnp.sum(lf, axis=-1, keepdims=True), (rows, page))
    suffix = within + jnp.dot(later_pages.astype(F32), tot, precision=HIGHEST, preferred_element_type=F32)

    qb = qblk.astype(BF16)
    cn = cn_ref[0]
    s_list = []
    for g in range(n_pages):
        kt = k_refs[g][0, 0].reshape(width, page).astype(BF16)
        s_list.append(_dot(qb, kt) + cn + suffix[g * n_heads:(g + 1) * n_heads])
    m = s_list[0]
    for s in s_list[1:]:
        m = jnp.maximum(m, s)
    m = jnp.maximum(jnp.max(m, axis=-1, keepdims=True), s_new)
    p_new = jnp.exp(s_new - m)
    l = p_new
    acc = p_new * vn_ref[0]
    for g in range(n_pages):
        p = jnp.exp(s_list[g] - m)
        l = l + jnp.sum(p, axis=-1, keepdims=True)
        vt = v_refs[g][0, 0].reshape(width, page).astype(BF16)
        acc = acc + _dot_nt(p.astype(BF16), vt)
    o_ref[0] = jnp.sum(jnp.where(diag, acc / l, 0.0), axis=0, keepdims=True)


def _fox_sample(page_table, cache_kt, cache_vt, cache_lft, fq, fk, fv, flf, *, layer_e):
    bs, n_pages = page_table.shape
    _, _, n_heads, _, page = cache_kt.shape
    width = n_heads * FOX_DIM

    def kv_spec(g):
        return pl.BlockSpec((1, 1, n_heads, FOX_DIM, page),
                            lambda b, pt: (layer_e, pt[b * n_pages + g], 0, 0, 0))

    def lf_spec(g):
        return pl.BlockSpec((1, 1, n_heads, page), lambda b, pt: (layer_e, pt[b * n_pages + g], 0, 0))

    tok = pl.BlockSpec((1, 1, width), lambda b, pt: (b, 0, 0))
    r3 = lambda a: a.reshape(bs, 1, width)
    grid_spec = pltpu.PrefetchScalarGridSpec(
        num_scalar_prefetch=1, grid=(bs,),
        in_specs=[kv_spec(g) for g in range(n_pages)] * 2 + [lf_spec(g) for g in range(n_pages)] + [
            tok, tok, tok, pl.BlockSpec((1, n_heads, 1), lambda b, pt: (b, 0, 0))],
        out_specs=tok)
    o = pl.pallas_call(
        functools.partial(_fox_sample_body, n_pages=n_pages, n_heads=n_heads),
        grid_spec=grid_spec,
        out_shape=jax.ShapeDtypeStruct((bs, 1, width), F32),
        compiler_params=_cparams(("arbitrary",)),
        name="fox_sample",
    )(page_table.reshape(-1), *([cache_kt] * n_pages), *([cache_vt] * n_pages), *([cache_lft] * n_pages),
      r3(fq), r3(fk), r3(fv), flf.reshape(bs, n_heads, 1))
    return o.reshape(bs, width)


def _outproj_body(x_ref, a_ref, b_ref, w_ref, o_ref):
    ka = a_ref.shape[1]
    o_ref[...] = (x_ref[...] + _dot(a_ref[...].astype(BF16), w_ref[:ka, :])
                  + _dot(b_ref[...].astype(BF16), w_ref[ka:, :]))


def _outproj(x, a, b, w, *, tm):
    m, d = x.shape
    row = lambda n: pl.BlockSpec((tm, n), lambda i: (i, 0))
    return pl.pallas_call(
        _outproj_body, grid=(m // tm,),
        in_specs=[row(d), row(a.shape[1]), row(b.shape[1]), _full(w.shape)],
        out_specs=row(d), out_shape=jax.ShapeDtypeStruct((m, d), F32),
        compiler_params=_cparams(("arbitrary",)), name="outproj",
    )(x, a, b, w)


def _gelu(x):
    return 0.5 * x * (1.0 + lax.erf(x * (2.0 ** -0.5)))


def _ffn_cols(ffn_dim):
    cw = 256
    while ffn_dim % cw:
        cw //= 2
    return cw


def _ffn_prompt_body(x_ref, g_ref, win_ref, wdw_ref, bdw_ref, wout_ref, gfin_ref,
                     o_ref, hist_ref, ext_ref, carry_ref, act_ref, *, tm, ffn_dim, cw, final_norm):
    t = pl.program_id(1)

    @pl.when(t == 0)
    def _():
        carry_ref[...] = jnp.zeros_like(carry_ref)

    x = x_ref[...]
    h = _rmsnorm(x, g_ref[...]).astype(BF16)
    for j in range(ffn_dim // cw):
        cs = slice(j * cw, (j + 1) * cw)
        g = _dot(h, win_ref[:, cs])
        u = _dot(h, win_ref[:, ffn_dim + j * cw:ffn_dim + (j + 1) * cw])
        ext_ref[6:8, :] = carry_ref[6:8, cs]
        ext_ref[8:, :] = g
        carry_ref[6:8, cs] = g[tm - 2:, :]
        gc = (wdw_ref[0:1, cs] * ext_ref[6:6 + tm, :] + wdw_ref[1:2, cs] * ext_ref[7:7 + tm, :]
              + wdw_ref[2:3, cs] * g + bdw_ref[:, cs])
        act_ref[:, cs] = (_gelu(gc) * u).astype(BF16)
    y = x + _dot(act_ref[...], wout_ref[...])
    if final_norm:
        y = _rmsnorm(y, gfin_ref[...])
    o_ref[...] = y

    @pl.when(t == pl.num_programs(1) - 1)
    def _():
        hist_ref[0] = carry_ref[6:8, :]


def _ffn_prompt(x, g, w_in, w_dw, b_dw, w_out, g_final, *, batch, seq_len, tm, final_norm):
    m, d = x.shape
    ffn_dim = w_out.shape[0]
    cw = _ffn_cols(ffn_dim)
    nt = seq_len // tm
    row = pl.BlockSpec((tm, d), lambda b, t: (b * nt + t, 0))
    return pl.pallas_call(
        functools.partial(_ffn_prompt_body, tm=tm, ffn_dim=ffn_dim, cw=cw, final_norm=final_norm),
        grid=(batch, nt),
        in_specs=[row, _full((1, d)), _full(w_in.shape), _full(w_dw.shape), _full((1, ffn_dim)),
                  _full(w_out.shape), _full((1, d))],
        out_specs=[row, pl.BlockSpec((1, 2, ffn_dim), lambda b, t: (b, 0, 0))],
        out_shape=[jax.ShapeDtypeStruct((m, d), F32), jax.ShapeDtypeStruct((batch, 2, ffn_dim), F32)],
        scratch_shapes=[pltpu.VMEM((tm + 8, cw), F32), pltpu.VMEM((8, ffn_dim), F32),
                        pltpu.VMEM((tm, ffn_dim), BF16)],
        compiler_params=_cparams(("arbitrary", "arbitrary")),
        name="ffn_prompt",
    )(x, g, w_in, w_dw, b_dw, w_out, g_final)


def _ffn_sample_body(x_ref, g_ref, win_ref, wdw_ref, bdw_ref, wout_ref, gfin_ref, hist_ref,
                     o_ref, nh_ref, act_ref, *, ffn_dim, cw, final_norm):
    x = x_ref[...]
    h = _rmsnorm(x, g_ref[...]).astype(BF16)
    for j in range(ffn_dim // cw):
        cs = slice(j * cw, (j + 1) * cw)
        g = _dot(h, win_ref[:, cs])
        u = _dot(h, win_ref[:, ffn_dim + j * cw:ffn_dim + (j + 1) * cw])
        h0 = hist_ref[:, cs]
        h1 = hist_ref[:, ffn_dim + j * cw:ffn_dim + (j + 1) * cw]
        gc = wdw_ref[0:1, cs] * h0 + wdw_ref[1:2, cs] * h1 + wdw_ref[2:3, cs] * g + bdw_ref[:, cs]
        act_ref[:, cs] = (_gelu(gc) * u).astype(BF16)
        nh_ref[:, cs] = h1
        nh_ref[:, ffn_dim + j * cw:ffn_dim + (j + 1) * cw] = g
    y = x + _dot(act_ref[...], wout_ref[...])
    if final_norm:
        y = _rmsnorm(y, gfin_ref[...])
    o_ref[...] = y


def _ffn_sample(x, g, w_in, w_dw, b_dw, w_out, g_final, hist, *, final_norm):
    m, d = x.shape
    ffn_dim = w_out.shape[0]
    cw = _ffn_cols(ffn_dim)
    hist2 = hist.reshape(m, 2 * ffn_dim)
    y, nh = pl.pallas_call(
        functools.partial(_ffn_sample_body, ffn_dim=ffn_dim, cw=cw, final_norm=final_norm),
        grid=(1,),
        in_specs=[_full((m, d)), _full((1, d)), _full(w_in.shape), _full(w_dw.shape),
                  _full((1, ffn_dim)), _full(w_out.shape), _full((1, d)), _full(hist2.shape)],
        out_specs=[_full((m, d)), _full(hist2.shape)],
        out_shape=[jax.ShapeDtypeStruct((m, d), F32), jax.ShapeDtypeStruct(hist2.shape, F32)],
        scratch_shapes=[pltpu.VMEM((m, ffn_dim), BF16)],
        compiler_params=_cparams(("arbitrary",)),
        name="ffn_sample",
    )(x, g, w_in, w_dw, b_dw, w_out, g_final, hist2)
    return y, nh.reshape(m, 2, ffn_dim)


def _layernorm_silu(d, g, b):
    dc = d - jnp.mean(d, axis=-1, keepdims=True)
    y = dc * lax.rsqrt(jnp.mean(dc * dc, axis=-1, keepdims=True) + EPS) * g + b
    return _silu(y)


def _conf_prompt_body(x_ref, g_ref, w1_ref, b1_ref, wdw_ref, bdw_ref, lng_ref, lnb_ref, w2_ref, b2_ref,
                      o_ref, hist_ref, ext_ref, *, tm, ch, cw, pad):
    t = pl.program_id(1)

    @pl.when(t == 0)
    def _():
        ext_ref[0:pad, :] = jnp.zeros((pad, ch), F32)

    x = x_ref[...]
    h = _rmsnorm(x, g_ref[...]).astype(BF16)
    a = _dot(h, w1_ref[:, :ch]) + b1_ref[:, :ch]
    gt = _dot(h, w1_ref[:, ch:]) + b1_ref[:, ch:]
    ext_ref[pad:, :] = a * jax.nn.sigmoid(gt)
    off = pad - (cw - 1)
    d = bdw_ref[...] + wdw_ref[0:1, :] * ext_ref[off:off + tm, :]
    for j in range(1, cw):
        d = d + wdw_ref[j:j + 1, :] * ext_ref[off + j:off + j + tm, :]
    y = _layernorm_silu(d, lng_ref[...], lnb_ref[...]).astype(BF16)
    o_ref[...] = x + _dot(y, w2_ref[...]) + b2_ref[...]

    @pl.when(t == pl.num_programs(1) - 1)
    def _():
        hist_ref[0] = ext_ref[tm + off:tm + pad, :]

    ext_ref[0:pad, :] = ext_ref[tm:tm + pad, :]


def _conf_prompt(x, g, w1, b1, wdw, bdw, lng, lnb, w2, b2, *, batch, seq_len, tm):
    m, d = x.shape
    cw, ch = wdw.shape
    pad = -(-(cw - 1) // 8) * 8
    nt = seq_len // tm
    row = pl.BlockSpec((tm, d), lambda b, t: (b * nt + t, 0))
    return pl.pallas_call(
        functools.partial(_conf_prompt_body, tm=tm, ch=ch, cw=cw, pad=pad),
        grid=(batch, nt),
        in_specs=[row, _full((1, d)), _full(w1.shape), _full((1, 2 * ch)), _full(wdw.shape), _full((1, ch)),
                  _full((1, ch)), _full((1, ch)), _full(w2.shape), _full((1, d))],
        out_specs=[row, pl.BlockSpec((1, cw - 1, ch), lambda b, t: (b, 0, 0))],
        out_shape=[jax.ShapeDtypeStruct((m, d), F32), jax.ShapeDtypeStruct((batch, cw - 1, ch), F32)],
        scratch_shapes=[pltpu.VMEM((tm + pad, ch), F32)],
        compiler_params=_cparams(("arbitrary", "arbitrary")),
        name="conf_prompt",
    )(x, g, w1, b1, wdw, bdw, lng, lnb, w2, b2)


def _conf_sample_body(x_ref, g_ref, w1_ref, b1_ref, wdw_ref, bdw_ref, lng_ref, lnb_ref, w2_ref, b2_ref,
                      hist_ref, o_ref, nh_ref, *, ch, cw):
    x = x_ref[...]
    h = _rmsnorm(x, g_ref[...]).astype(BF16)
    a = _dot(h, w1_ref[:, :ch]) + b1_ref[:, :ch]
    gt = _dot(h, w1_ref[:, ch:]) + b1_ref[:, ch:]
    u = a * jax.nn.sigmoid(gt)
    d = bdw_ref[...] + wdw_ref[cw - 1:cw, :] * u
    for j in range(cw - 1):
        d = d + wdw_ref[j:j + 1, :] * hist_ref[j]
    nh_ref[0:cw - 2] = hist_ref[1:cw - 1]
    nh_ref[cw - 2] = u
    y = _layernorm_silu(d, lng_ref[...], lnb_ref[...]).astype(BF16)
    o_ref[...] = x + _dot(y, w2_ref[...]) + b2_ref[...]


def _conf_sample(x, g, w1, b1, wdw, bdw, lng, lnb, w2, b2, hist_t, *, bb):
    m, d = x.shape
    cw, ch = wdw.shape
    row = lambda n: pl.BlockSpec((bb, n), lambda i: (i, 0))
    hs = pl.BlockSpec((cw - 1, bb, ch), lambda i: (0, i, 0))
    return pl.pallas_call(
        functools.partial(_conf_sample_body, ch=ch, cw=cw),
        grid=(m // bb,),
        in_specs=[row(d), _full((1, d)), _full(w1.shape), _full((1, 2 * ch)), _full(wdw.shape), _full((1, ch)),
                  _full((1, ch)), _full((1, ch)), _full(w2.shape), _full((1, d)), hs],
        out_specs=[row(d), hs],
        out_shape=[jax.ShapeDtypeStruct((m, d), F32), jax.ShapeDtypeStruct(hist_t.shape, F32)],
        compiler_params=_cparams(("arbitrary",)),
        name="conf_sample",
    )(x, g, w1, b1, wdw, bdw, lng, lnb, w2, b2, hist_t)


def kernel(x_prompt, x_sample, cache_fox_k, cache_fox_v, cache_fox_logf, page_table, state_hgrn, state_conv,
           state_ffn_conv, norm_mix, norm_ffn, norm_final, w_in0, fox_fb, hg_lb, hg_gnorm, w_out0, w_pw1, b_pw1,
           w_dw, b_dw, ln_g, ln_b, w_pw2, b_pw2, w_ffn_in, w_ffn_dw, b_ffn_dw, w_ffn_out):
    bp, seq_len, d = x_prompt.shape
    bs, dec_seq, _ = x_sample.shape
    assert dec_seq == 1, "the sample path handles one new token per sequence"
    depth = norm_mix.shape[0]
    n_fox_heads = fox_fb.shape[1]
    fox_w = n_fox_heads * FOX_DIM
    hg_w = hg_gnorm.shape[1]
    hg_heads = hg_w // HG_DIM
    page = cache_fox_k.shape[2]
    row2 = lambda a: a.reshape(1, -1)

    xp = x_prompt.reshape(bp * seq_len, d)
    xs = x_sample.reshape(bs, d)
    tm = _tile(seq_len, 512)

    fk_p, fv_p, fl_p, fk_s, fv_s, fl_s = [], [], [], [], [], []
    hs_p, hs_s, cs_p, cs_s, ffs_p, ffs_s = [], [], [], [], [], []
    for l in range(depth):
        if l % 2 == 0:
            e = l // 2
            w_main = w_in0[e][:, :7 * hg_w].astype(BF16)
            w_kvt = w_in0[e][:, 5 * hg_w:7 * hg_w].T.astype(BF16)
            w_ff = w_in0[e][:, 7 * hg_w:].astype(BF16)
            w_out = w_out0[e].astype(BF16)
            gn = row2(hg_gnorm[e])
            fb = row2(fox_fb[e])
            q, fr, v, gate, fq, fkt, fvt, flft, c, ct = _inproj_prompt(
                xp, row2(norm_mix[l]), w_main, w_kvt, w_ff, fb, hg_lb, layer_e=e, batch=bp, seq_len=seq_len, tm=tm)
            o_hg, s_fin = _hgrn_prompt(q, fr, v, gate, gn, batch=bp, seq_len=seq_len, tc=tm)
            o_fox = _fox_prompt(fq, fkt, fvt, c, ct, batch=bp, seq_len=seq_len, tq=tm)
            xp = _outproj(xp, o_hg, o_fox, w_out, tm=tm)
            to_heads = lambda a: a.reshape(bp, n_fox_heads, FOX_DIM, seq_len).transpose(0, 3, 1, 2)
            fk_p.append(to_heads(fkt))
            fv_p.append(to_heads(fvt))
            fl_p.append(flft.transpose(0, 2, 1))
            hs_p.append(s_fin)
            q, fr, v, gate, fq, fk, fv, flf = _inproj_sample(
                xs, row2(norm_mix[l]), w_main, w_ff, fb, hg_lb, layer_e=e)
            o_hg, s_fin = _hgrn_sample(q, fr, v, gate, gn, state_hgrn[e], bb=8)
            o_fox = _fox_sample(page_table, cache_fox_k.transpose(0, 1, 3, 4, 2),
                                cache_fox_v.transpose(0, 1, 3, 4, 2), cache_fox_logf.transpose(0, 1, 3, 2),
                                fq, fk, fv, flf, layer_e=e)
            xs = _outproj(xs, o_hg, o_fox, w_out, tm=bs)
            fk_s.append(fk.reshape(bs, 1, n_fox_heads, FOX_DIM))
            fv_s.append(fv.reshape(bs, 1, n_fox_heads, FOX_DIM))
            fl_s.append(flf.reshape(bs, 1, n_fox_heads))
            hs_s.append(s_fin)
        else:
            o = l // 2
            args = (row2(norm_mix[l]), w_pw1[o].astype(BF16), row2(b_pw1[o]), w_dw[o], row2(b_dw[o]),
                    row2(ln_g[o]), row2(ln_b[o]), w_pw2[o].astype(BF16), row2(b_pw2[o]))
            xp, hist = _conf_prompt(xp, *args, batch=bp, seq_len=seq_len, tm=_tile(seq_len, 256))
            cs_p.append(hist)
            xs, hist_t = _conf_sample(xs, *args, state_conv[o].transpose(1, 0, 2), bb=32)
            cs_s.append(hist_t.transpose(1, 0, 2))
        last = l == depth - 1
        fargs = (row2(norm_ffn[l]), w_ffn_in[l].astype(BF16), w_ffn_dw[l], row2(b_ffn_dw[l]),
                 w_ffn_out[l].astype(BF16), row2(norm_final))
        xp, hist = _ffn_prompt(xp, *fargs, batch=bp, seq_len=seq_len, tm=tm, final_norm=last)
        ffs_p.append(hist)
        xs, hist = _ffn_sample(xs, *fargs, state_ffn_conv[l], final_norm=last)
        ffs_s.append(hist)

    return (xp.reshape(bp, seq_len, d), xs.reshape(bs, 1, d),
            jnp.stack(fk_p), jnp.stack(fv_p), jnp.stack(fl_p),
            jnp.stack(fk_s), jnp.stack(fv_s), jnp.stack(fl_s),
            jnp.stack(hs_p), jnp.stack(hs_s), jnp.stack(cs_p), jnp.stack(cs_s),
            jnp.stack(ffs_p), jnp.stack(ffs_s))
```

```python
import functools
import math

import numpy as np
import jax
import jax.numpy as jnp
from jax import lax
from jax.experimental import pallas as pl
from jax.experimental.pallas import tpu as pltpu

F32 = jnp.float32
BF16 = jnp.bfloat16
EPS = 1e-6
HG_DIM = 128
HG_CHUNK = 64
FOX_DIM = 64
LANES = 128
SUBLANES = 8
VMEM_LIMIT = 56 * 1024 * 1024
HIGHEST = lax.Precision.HIGHEST
LOG2E = math.log2(math.e)
BF16_ROWS = 16


def _cparams(sem):
    return pltpu.CompilerParams(dimension_semantics=sem, vmem_limit_bytes=VMEM_LIMIT)


def _tile(n, target):
    t = min(n, target)
    while n % t:
        t -= 8
    return t


def _rmsnorm(x, g):
    return x * lax.rsqrt(jnp.mean(x * x, axis=-1, keepdims=True) + EPS) * g


def _silu(x):
    return x * jax.nn.sigmoid(x)


def _log_sigmoid(x):
    return -(jnp.maximum(-x, 0.0) + jnp.log1p(jnp.exp(-jnp.abs(x))))


def _dot(a, b):
    return jnp.dot(a, b, preferred_element_type=F32)


def _dot_nt(a, b):
    return lax.dot_general(a, b, (((1,), (1,)), ((), ())), preferred_element_type=F32)


def _full(shape):
    return pl.BlockSpec(shape, lambda *_: (0,) * len(shape))


def _inproj_common(x_ref, g_ref, w_ref, lb_ref, q_ref, fr_ref, v_ref, gate_ref, *, layer_e, hgw):
    h = _rmsnorm(x_ref[...], g_ref[...]).astype(BF16)

    def proj(k):
        return _dot(h, w_ref[:, k * hgw:(k + 1) * hgw])

    lbp = lb_ref[...]
    ex = jnp.exp(lbp - jnp.max(lbp, axis=0, keepdims=True))
    sm = ex / jnp.sum(ex, axis=0, keepdims=True)
    lb = jnp.sum(sm[:layer_e + 1], axis=0, keepdims=True)

    q_ref[...] = _silu(proj(0))
    fr_ref[...] = lb + (1.0 - lb) * jax.nn.sigmoid(proj(1))
    v_ref[...] = proj(2)
    gate_ref[...] = _silu(proj(3))
    return h, proj


def _split3(x):
    hi = x.astype(BF16)
    r = x - hi.astype(F32)
    mid = r.astype(BF16)
    return hi, mid, (r - mid.astype(F32)).astype(BF16)


def _inproj_prompt_body(x_ref, g_ref, w_ref, wqkvt_ref, wff_ref, wfft_ref, fb_ref, fbt_ref, lb_ref, tri_ref,
                        trit_ref, place_ref, q_ref, fr_ref, v_ref, gate_ref, fqt_ref, fk_ref, fkt_ref, fvt_ref,
                        flft_ref, ct_ref, caug_ref, carry_ref, carryt_ref, *, layer_e, hgw):
    h, proj = _inproj_common(x_ref, g_ref, w_ref, lb_ref, q_ref, fr_ref, v_ref, gate_ref,
                             layer_e=layer_e, hgw=hgw)
    fk_ref[...] = proj(5)
    qkvt = _dot_nt(wqkvt_ref[...], h)
    fqt_ref[0] = qkvt[0:hgw]
    fkt_ref[0] = qkvt[hgw:2 * hgw]
    fvt_ref[0] = qkvt[2 * hgw:]

    @pl.when(pl.program_id(1) == 0)
    def _():
        carry_ref[...] = jnp.zeros_like(carry_ref)
        carryt_ref[...] = jnp.zeros_like(carryt_ref)

    flf = _log_sigmoid(_dot(h, wff_ref[...]) + fb_ref[...])
    c = jnp.dot(tri_ref[...], flf, precision=HIGHEST, preferred_element_type=F32) + carry_ref[...]
    carry_ref[...] = c[-1:, :]
    parts = _split3(c * LOG2E)
    caug_ref[...] = sum(_dot(p, place_ref[i]) for i, p in enumerate(parts)).astype(BF16)
    flft = _log_sigmoid(_dot_nt(wfft_ref[...], h) + fbt_ref[...])
    flft_ref[0] = flft
    ct = jnp.dot(flft, trit_ref[...], precision=HIGHEST, preferred_element_type=F32) + carryt_ref[...]
    ct_ref[0] = ct
    carryt_ref[...] = ct[:, -1:]


def _caug_lane(head, part):
    return LANES * (head // 2) + 3 * (head % 2) + part


def _inproj_prompt(x, g, w_main, w_qkvt, w_ff, fb, hg_lb, *, layer_e, batch, seq_len, tm):
    m, d = x.shape
    hgw = w_qkvt.shape[0] // 3
    nfh = w_ff.shape[1]
    nt = seq_len // tm
    tri = np.tril(np.ones((tm, tm), np.float32))
    place = np.zeros((3, nfh, hgw), np.float32)
    for part in range(3):
        for head in range(nfh):
            place[part, head, _caug_lane(head, part)] = 1.0
    row = lambda n: pl.BlockSpec((tm, n), lambda b, t: (b * nt + t, 0))
    tr = lambda n: pl.BlockSpec((1, n, tm), lambda b, t: (b, 0, t))
    wide = jax.ShapeDtypeStruct((m, hgw), F32)
    wide_t = jax.ShapeDtypeStruct((batch, hgw, seq_len), F32)
    narrow_t = jax.ShapeDtypeStruct((batch, nfh, seq_len), F32)
    return pl.pallas_call(
        functools.partial(_inproj_prompt_body, layer_e=layer_e, hgw=hgw),
        grid=(batch, nt),
        in_specs=[row(d), _full((1, d)), _full(w_main.shape), _full(w_qkvt.shape), _full(w_ff.shape),
                  _full((nfh, d)), _full((1, nfh)), _full((nfh, 1)), _full(hg_lb.shape), _full((tm, tm)),
                  _full((tm, tm)), _full(place.shape)],
        out_specs=[row(hgw)] * 4 + [tr(hgw), row(hgw), tr(hgw), tr(hgw), tr(nfh), tr(nfh), row(hgw)],
        out_shape=[wide] * 4 + [wide_t, wide, wide_t, wide_t, narrow_t, narrow_t,
                                jax.ShapeDtypeStruct((m, hgw), BF16)],
        scratch_shapes=[pltpu.VMEM((1, nfh), F32), pltpu.VMEM((nfh, 1), F32)],
        compiler_params=_cparams(("arbitrary", "arbitrary")),
        name="inproj_prompt",
    )(x, g, w_main, w_qkvt, w_ff, w_ff.T, fb, fb.reshape(nfh, 1), hg_lb, jnp.asarray(tri), jnp.asarray(tri.T),
      jnp.asarray(place, BF16))


def _inproj_sample_body(x_ref, g_ref, w_ref, wff_ref, fb_ref, lb_ref,
                        q_ref, fr_ref, v_ref, gate_ref, fq_ref, fk_ref, fv_ref, flf_ref, *, layer_e, hgw):
    h, proj = _inproj_common(x_ref, g_ref, w_ref, lb_ref, q_ref, fr_ref, v_ref, gate_ref,
                             layer_e=layer_e, hgw=hgw)
    fq_ref[...] = proj(4)
    fk_ref[...] = proj(5)
    fv_ref[...] = proj(6)
    flf_ref[...] = _log_sigmoid(_dot(h, wff_ref[...]) + fb_ref[...])


def _inproj_sample(x, g, w_main, w_ff, fb, hg_lb, *, layer_e):
    m, d = x.shape
    hgw = w_main.shape[1] // 7
    nfh = w_ff.shape[1]
    wide = jax.ShapeDtypeStruct((m, hgw), F32)
    return pl.pallas_call(
        functools.partial(_inproj_sample_body, layer_e=layer_e, hgw=hgw),
        grid=(1,),
        in_specs=[_full((m, d)), _full((1, d)), _full(w_main.shape), _full(w_ff.shape), _full((1, nfh)),
                  _full(hg_lb.shape)],
        out_specs=[_full((m, hgw))] * 7 + [_full((m, nfh))],
        out_shape=[wide] * 7 + [jax.ShapeDtypeStruct((m, nfh), F32)],
        compiler_params=_cparams(("arbitrary",)),
        name="inproj_sample",
    )(x, g, w_main, w_ff, fb, hg_lb)


def _hgrn_prompt_body(q_ref, fr_ref, v_ref, gate_ref, gn_ref, tri_ref, o_ref, s_ref, st_ref,
                      *, n_chunks, n_heads):
    t = pl.program_id(1)

    @pl.when(t == 0)
    def _():
        st_ref[...] = jnp.zeros_like(st_ref)

    tri = tri_ref[...]
    rows = lax.broadcasted_iota(jnp.int32, (HG_CHUNK, HG_CHUNK), 0)
    cols = lax.broadcasted_iota(jnp.int32, (HG_CHUNK, HG_CHUNK), 1)
    causal = rows >= cols
    gn = gn_ref[...]

    def chunk(c, carry):
        r0 = pl.multiple_of(c * HG_CHUNK, HG_CHUNK)
        rs = pl.ds(r0, HG_CHUNK)
        fr = fr_ref[rs, :]
        q = q_ref[rs, :]
        v = v_ref[rs, :]
        gate = gate_ref[rs, :]
        k = 1.0 - fr
        b = jnp.dot(tri, jnp.log(fr), precision=HIGHEST, preferred_element_type=F32)
        b_last = b[HG_CHUNK - 1:HG_CHUNK, :]
        b_mid = b[HG_CHUNK // 2:HG_CHUNK // 2 + 1, :]
        kdec = (k * jnp.exp(b_last - b)).astype(BF16)
        qb = (q * jnp.exp(b)).astype(BF16)
        qr = (q * jnp.exp(b - b_mid)).astype(BF16)
        kr = (k * jnp.exp(b_mid - b)).astype(BF16)
        dec = jnp.exp(b_last)
        for h in range(n_heads):
            sl = slice(h * HG_DIM, (h + 1) * HG_DIM)
            st = st_ref[h]
            o_inter = _dot_nt(qb[:, sl], st.astype(BF16))
            att = jnp.where(causal, _dot_nt(qr[:, sl], kr[:, sl]), 0.0)
            vh = v[:, sl]
            o = o_inter + _dot(att.astype(BF16), vh.astype(BF16))
            st_ref[h] = st * dec[:, sl] + _dot(vh.T.astype(BF16), kdec[:, sl])
            o = o * lax.rsqrt(jnp.mean(o * o, axis=-1, keepdims=True) + EPS)
            o_ref[rs, sl] = (o * gn[:, sl] * gate[:, sl]).astype(o_ref.dtype)
        return carry

    lax.fori_loop(0, n_chunks, chunk, 0)

    @pl.when(t == pl.num_programs(1) - 1)
    def _():
        for h in range(n_heads):
            s_ref[0, h] = st_ref[h].T


def _hgrn_prompt(q, fr, v, gate, gn, *, batch, seq_len, tc):
    m, hgw = q.shape
    n_heads = hgw // HG_DIM
    nt = seq_len // tc
    tri = jnp.asarray(np.tril(np.ones((HG_CHUNK, HG_CHUNK), np.float32)))
    row = pl.BlockSpec((tc, hgw), lambda b, t: (b * nt + t, 0))
    return pl.pallas_call(
        functools.partial(_hgrn_prompt_body, n_chunks=tc // HG_CHUNK, n_heads=n_heads),
        grid=(batch, nt),
        in_specs=[row, row, row, row, _full((1, hgw)), _full((HG_CHUNK, HG_CHUNK))],
        out_specs=[row, pl.BlockSpec((1, n_heads, HG_DIM, HG_DIM), lambda b, t: (b, 0, 0, 0))],
        out_shape=[jax.ShapeDtypeStruct((m, hgw), BF16),
                   jax.ShapeDtypeStruct((batch, n_heads, HG_DIM, HG_DIM), F32)],
        scratch_shapes=[pltpu.VMEM((n_heads, HG_DIM, HG_DIM), F32)],
        compiler_params=_cparams(("arbitrary", "arbitrary")),
        name="hgrn_prompt",
    )(q, fr, v, gate, gn, tri)


def _hgrn_sample_body(q_ref, fr_ref, v_ref, gate_ref, gn_ref, s_ref, o_ref, so_ref, *, bb, n_heads):
    def col(r):
        return jnp.broadcast_to(r, (HG_DIM, HG_DIM)).T

    for i in range(bb):
        for h in range(n_heads):
            sl = slice(h * HG_DIM, (h + 1) * HG_DIM)
            fr = fr_ref[i:i + 1, sl]
            dec = jnp.exp(jnp.log(fr))
            s_new = col(dec) * s_ref[i, h] + col(1.0 - fr) * v_ref[i:i + 1, sl]
            so_ref[i, h] = s_new
            o = jnp.sum(col(q_ref[i:i + 1, sl]) * s_new, axis=0, keepdims=True)
            o = o * lax.rsqrt(jnp.mean(o * o, axis=-1, keepdims=True) + EPS)
            o_ref[i:i + 1, sl] = (o * gn_ref[:, sl] * gate_ref[i:i + 1, sl]).astype(o_ref.dtype)


def _hgrn_sample(q, fr, v, gate, gn, state, *, bb):
    m, hgw = q.shape
    n_heads = hgw // HG_DIM
    row = pl.BlockSpec((bb, hgw), lambda i: (i, 0))
    st = pl.BlockSpec((bb, n_heads, HG_DIM, HG_DIM), lambda i: (i, 0, 0, 0))
    return pl.pallas_call(
        functools.partial(_hgrn_sample_body, bb=bb, n_heads=n_heads),
        grid=(m // bb,),
        in_specs=[row, row, row, row, _full((1, hgw)), st],
        out_specs=[row, st],
        out_shape=[jax.ShapeDtypeStruct((m, hgw), BF16), jax.ShapeDtypeStruct(state.shape, F32)],
        compiler_params=_cparams(("arbitrary",)),
        name="hgrn_sample",
    )(q, fr, v, gate, gn, state)


def _fox_prompt_body(qi_ref, ki_ref, qt_ref, k_ref, ca_ref, vt_ref, ct_ref, o_ref, m_ref, acc_ref, *, tq):
    j = pl.program_id(2)
    qi = qi_ref[j]
    ki = ki_ref[j]

    @pl.when(ki == 0)
    def _():
        m_ref[...] = jnp.full_like(m_ref, -jnp.inf)
        acc_ref[...] = jnp.zeros_like(acc_ref)

    sub = lax.broadcasted_iota(jnp.int32, (LANES, 1), 0)

    def update(masked):
        qt = qt_ref[0] * (FOX_DIM ** -0.5 * LOG2E)
        kaug = jnp.concatenate([k_ref[0].astype(BF16), ca_ref[0]], axis=1)
        vaug = jnp.concatenate([vt_ref[0].astype(BF16), jnp.ones((BF16_ROWS, tq), BF16)], axis=0)
        if masked:
            keep = (lax.broadcasted_iota(jnp.int32, (tq, tq), 1)
                    >= lax.broadcasted_iota(jnp.int32, (tq, tq), 0))
        for hh in range(2):
            qm = jnp.where(sub // FOX_DIM == hh, qt, 0.0)
            neg = jnp.where((sub >= 3 * hh) & (sub < 3 * hh + 3), -1.0, 0.0)
            qaug = jnp.concatenate([qm, jnp.broadcast_to(neg, (LANES, tq))], axis=0).astype(BF16)
            s = _dot(kaug, qaug)
            if masked:
                s = jnp.where(keep, s, -jnp.inf)
            ct = ct_ref[0, 0, hh:hh + 1, :] * LOG2E
            m_prev = m_ref[hh]
            m_new = jnp.maximum(m_prev, jnp.max(s, axis=0, keepdims=True) + ct)
            p = jnp.exp2(s - (m_new - ct))
            alpha = jnp.exp2(m_prev - m_new)
            acc_ref[hh] = alpha * acc_ref[hh] + _dot(vaug, p.astype(BF16))
            m_ref[hh] = m_new

    @pl.when(ki < qi)
    def _():
        update(False)

    @pl.when(ki == qi)
    def _():
        update(True)
        a0 = acc_ref[0]
        a1 = acc_ref[1]
        o0 = a0[:LANES] * (1.0 / a0[LANES:LANES + 1])
        o1 = a1[:LANES] * (1.0 / a1[LANES:LANES + 1])
        o_ref[0] = jnp.where(sub // FOX_DIM == 0, o0, o1).T.astype(o_ref.dtype)


def _fox_prompt(fqt, fk, caug, fvt, ct, *, batch, seq_len, tq):
    m, fw = fk.shape
    n_pairs = fw // LANES
    nq = seq_len // tq
    qi_tab = np.array([q for q in range(nq) for _ in range(q + 1)], np.int32)
    ki_tab = np.array([k for q in range(nq) for k in range(q + 1)], np.int32)
    r3 = lambda a: a.reshape(batch, seq_len, fw)
    by_q = lambda b, p, j, qt, kt: (b, p, qt[j])
    by_k = lambda b, p, j, qt, kt: (b, kt[j], p)
    grid_spec = pltpu.PrefetchScalarGridSpec(
        num_scalar_prefetch=2,
        grid=(batch, n_pairs, len(qi_tab)),
        in_specs=[
            pl.BlockSpec((1, LANES, tq), by_q),
            pl.BlockSpec((1, tq, LANES), by_k),
            pl.BlockSpec((1, tq, LANES), by_k),
            pl.BlockSpec((1, LANES, tq), lambda b, p, j, qt, kt: (b, p, kt[j])),
            pl.BlockSpec((1, 1, 2, tq), lambda b, p, j, qt, kt: (b, p, 0, qt[j])),
        ],
        out_specs=pl.BlockSpec((1, tq, LANES), lambda b, p, j, qt, kt: (b, qt[j], p)),
        scratch_shapes=[pltpu.VMEM((2, 1, tq), F32), pltpu.VMEM((2, LANES + BF16_ROWS, tq), F32)],
    )
    o = pl.pallas_call(
        functools.partial(_fox_prompt_body, tq=tq),
        grid_spec=grid_spec,
        out_shape=jax.ShapeDtypeStruct((batch, seq_len, fw), BF16),
        compiler_params=_cparams(("arbitrary", "arbitrary", "arbitrary")),
        name="fox_prompt",
    )(jnp.asarray(qi_tab), jnp.asarray(ki_tab), fqt, r3(fk), r3(caug), fvt,
      ct.reshape(batch, n_pairs, 2, seq_len))
    return o.reshape(m, fw)


def _fox_sample_body(pt_ref, *refs, n_pages, n_heads):
    k_refs, v_refs, lf_refs = refs[:n_pages], refs[n_pages:2 * n_pages], refs[2 * n_pages:3 * n_pages]
    q_ref, kn_ref, vn_ref, cn_ref, o_ref = refs[3 * n_pages:]
    page = lf_refs[0].shape[-1]
    width = n_heads * FOX_DIM
    rows = n_pages * n_heads
    scale = FOX_DIM ** -0.5

    diag = (lax.broadcasted_iota(jnp.int32, (n_heads, width), 1) // FOX_DIM
            == lax.broadcasted_iota(jnp.int32, (n_heads, width), 0))
    qblk = jnp.where(diag, q_ref[0] * scale, 0.0)
    s_new = jnp.sum(qblk * kn_ref[0], axis=-1, keepdims=True)

    lf = jnp.concatenate([r[0, 0] for r in lf_refs], axis=0)
    pj = lax.broadcasted_iota(jnp.int32, (page, page), 0)
    ps = lax.broadcasted_iota(jnp.int32, (page, page), 1)
    within = jnp.dot(lf, jnp.where(pj > ps, 1.0, 0.0).astype(F32), precision=HIGHEST,
                     preferred_element_type=F32)
    rr = lax.broadcasted_iota(jnp.int32, (rows, rows), 0)
    rc = lax.broadcasted_iota(jnp.int32, (rows, rows), 1)
    later_pages = jnp.where(rc % n_heads == rr % n_heads, jnp.where(rc // n_heads > rr // n_heads, 1.0, 0.0), 0.0)
    tot = jnp.broadcast_to(jnp.sum(lf, axis=-1, keepdims=True), (rows, page))
    suffix = within + jnp.dot(later_pages.astype(F32), tot, precision=HIGHEST, preferred_element_type=F32)

    qb = qblk.astype(BF16)
    cn = cn_ref[0]
    s_list = []
    for g in range(n_pages):
        kt = k_refs[g][0, 0].reshape(width, page).astype(BF16)
        s_list.append(_dot(qb, kt) + cn + suffix[g * n_heads:(g + 1) * n_heads])
    m = s_list[0]
    for s in s_list[1:]:
        m = jnp.maximum(m, s)
    m = jnp.maximum(jnp.max(m, axis=-1, keepdims=True), s_new)
    p_new = jnp.exp(s_new - m)
    l = p_new
    acc = p_new * vn_ref[0]
    for g in range(n_pages):
        p = jnp.exp(s_list[g] - m)
        l = l + jnp.sum(p, axis=-1, keepdims=True)
        vt = v_refs[g][0, 0].reshape(width, page).astype(BF16)
        acc = acc + _dot_nt(p.astype(BF16), vt)
    o_ref[0] = jnp.sum(jnp.where(diag, acc / l, 0.0), axis=0, keepdims=True)


def _fox_sample(page_table, cache_kt, cache_vt, cache_lft, fq, fk, fv, flf, *, layer_e):
    bs, n_pages = page_table.shape
    _, _, n_heads, _, page = cache_kt.shape
    width = n_heads * FOX_DIM

    def kv_spec(g):
        return pl.BlockSpec((1, 1, n_heads, FOX_DIM, page),
                            lambda b, pt: (layer_e, pt[b * n_pages + g], 0, 0, 0))

    def lf_spec(g):
        return pl.BlockSpec((1, 1, n_heads, page), lambda b, pt: (layer_e, pt[b * n_pages + g], 0, 0))

    tok = pl.BlockSpec((1, 1, width), lambda b, pt: (b, 0, 0))
    r3 = lambda a: a.reshape(bs, 1, width)
    grid_spec = pltpu.PrefetchScalarGridSpec(
        num_scalar_prefetch=1, grid=(bs,),
        in_specs=[kv_spec(g) for g in range(n_pages)] * 2 + [lf_spec(g) for g in range(n_pages)] + [
            tok, tok, tok, pl.BlockSpec((1, n_heads, 1), lambda b, pt: (b, 0, 0))],
        out_specs=tok)
    o = pl.pallas_call(
        functools.partial(_fox_sample_body, n_pages=n_pages, n_heads=n_heads),
        grid_spec=grid_spec,
        out_shape=jax.ShapeDtypeStruct((bs, 1, width), F32),
        compiler_params=_cparams(("arbitrary",)),
        name="fox_sample",
    )(page_table.reshape(-1), *([cache_kt] * n_pages), *([cache_vt] * n_pages), *([cache_lft] * n_pages),
      r3(fq), r3(fk), r3(fv), flf.reshape(bs, n_heads, 1))
    return o.reshape(bs, width)


def _outproj_body(x_ref, a_ref, b_ref, w_ref, o_ref):
    ka = a_ref.shape[1]
    o_ref[...] = (x_ref[...] + _dot(a_ref[...].astype(BF16), w_ref[:ka, :])
                  + _dot(b_ref[...].astype(BF16), w_ref[ka:, :]))


def _outproj(x, a, b, w, *, tm):
    m, d = x.shape
    row = lambda n: pl.BlockSpec((tm, n), lambda i: (i, 0))
    return pl.pallas_call(
        _outproj_body, grid=(m // tm,),
        in_specs=[row(d), row(a.shape[1]), row(b.shape[1]), _full(w.shape)],
        out_specs=row(d), out_shape=jax.ShapeDtypeStruct((m, d), F32),
        compiler_params=_cparams(("arbitrary",)), name="outproj",
    )(x, a, b, w)


def _gelu(x):
    return 0.5 * x * (1.0 + lax.erf(x * (2.0 ** -0.5)))


def _ffn_cols(ffn_dim):
    cw = 256
    while ffn_dim % cw:
        cw //= 2
    return cw


def _ffn_prompt_body(x_ref, g_ref, win_ref, wdw_ref, bdw_ref, wout_ref, gfin_ref,
                     o_ref, hist_ref, ext_ref, carry_ref, act_ref, *, tm, ffn_dim, cw, final_norm):
    t = pl.program_id(1)

    @pl.when(t == 0)
    def _():
        carry_ref[...] = jnp.zeros_like(carry_ref)

    x = x_ref[...]
    h = _rmsnorm(x, g_ref[...]).astype(BF16)
    for j in range(ffn_dim // cw):
        cs = slice(j * cw, (j + 1) * cw)
        g = _dot(h, win_ref[:, cs])
        u = _dot(h, win_ref[:, ffn_dim + j * cw:ffn_dim + (j + 1) * cw])
        ext_ref[6:8, :] = carry_ref[6:8, cs]
        ext_ref[8:, :] = g
        carry_ref[6:8, cs] = g[tm - 2:, :]
        gc = (wdw_ref[0:1, cs] * ext_ref[6:6 + tm, :] + wdw_ref[1:2, cs] * ext_ref[7:7 + tm, :]
              + wdw_ref[2:3, cs] * g + bdw_ref[:, cs])
        act_ref[:, cs] = (_gelu(gc) * u).astype(BF16)
    y = x + _dot(act_ref[...], wout_ref[...])
    if final_norm:
        y = _rmsnorm(y, gfin_ref[...])
    o_ref[...] = y

    @pl.when(t == pl.num_programs(1) - 1)
    def _():
        hist_ref[0] = carry_ref[6:8, :]


def _ffn_prompt(x, g, w_in, w_dw, b_dw, w_out, g_final, *, batch, seq_len, tm, final_norm):
    m, d = x.shape
    ffn_dim = w_out.shape[0]
    cw = _ffn_cols(ffn_dim)
    nt = seq_len // tm
    row = pl.BlockSpec((tm, d), lambda b, t: (b * nt + t, 0))
    return pl.pallas_call(
        functools.partial(_ffn_prompt_body, tm=tm, ffn_dim=ffn_dim, cw=cw, final_norm=final_norm),
        grid=(batch, nt),
        in_specs=[row, _full((1, d)), _full(w_in.shape), _full(w_dw.shape), _full((1, ffn_dim)),
                  _full(w_out.shape), _full((1, d))],
        out_specs=[row, pl.BlockSpec((1, 2, ffn_dim), lambda b, t: (b, 0, 0))],
        out_shape=[jax.ShapeDtypeStruct((m, d), F32), jax.ShapeDtypeStruct((batch, 2, ffn_dim), F32)],
        scratch_shapes=[pltpu.VMEM((tm + 8, cw), F32), pltpu.VMEM((8, ffn_dim), F32),
                        pltpu.VMEM((tm, ffn_dim), BF16)],
        compiler_params=_cparams(("arbitrary", "arbitrary")),
        name="ffn_prompt",
    )(x, g, w_in, w_dw, b_dw, w_out, g_final)


def _ffn_sample_body(x_ref, g_ref, win_ref, wdw_ref, bdw_ref, wout_ref, gfin_ref, hist_ref,
                     o_ref, nh_ref, act_ref, *, ffn_dim, cw, final_norm):
    x = x_ref[...]
    h = _rmsnorm(x, g_ref[...]).astype(BF16)
    for j in range(ffn_dim // cw):
        cs = slice(j * cw, (j + 1) * cw)
        g = _dot(h, win_ref[:, cs])
        u = _dot(h, win_ref[:, ffn_dim + j * cw:ffn_dim + (j + 1) * cw])
        h0 = hist_ref[:, cs]
        h1 = hist_ref[:, ffn_dim + j * cw:ffn_dim + (j + 1) * cw]
        gc = wdw_ref[0:1, cs] * h0 + wdw_ref[1:2, cs] * h1 + wdw_ref[2:3, cs] * g + bdw_ref[:, cs]
        act_ref[:, cs] = (_gelu(gc) * u).astype(BF16)
        nh_ref[:, cs] = h1
        nh_ref[:, ffn_dim + j * cw:ffn_dim + (j + 1) * cw] = g
    y = x + _dot(act_ref[...], wout_ref[...])
    if final_norm:
        y = _rmsnorm(y, gfin_ref[...])
    o_ref[...] = y


def _ffn_sample(x, g, w_in, w_dw, b_dw, w_out, g_final, hist, *, final_norm):
    m, d = x.shape
    ffn_dim = w_out.shape[0]
    cw = _ffn_cols(ffn_dim)
    hist2 = hist.reshape(m, 2 * ffn_dim)
    y, nh = pl.pallas_call(
        functools.partial(_ffn_sample_body, ffn_dim=ffn_dim, cw=cw, final_norm=final_norm),
        grid=(1,),
        in_specs=[_full((m, d)), _full((1, d)), _full(w_in.shape), _full(w_dw.shape),
                  _full((1, ffn_dim)), _full(w_out.shape), _full((1, d)), _full(hist2.shape)],
        out_specs=[_full((m, d)), _full(hist2.shape)],
        out_shape=[jax.ShapeDtypeStruct((m, d), F32), jax.ShapeDtypeStruct(hist2.shape, F32)],
        scratch_shapes=[pltpu.VMEM((m, ffn_dim), BF16)],
        compiler_params=_cparams(("arbitrary",)),
        name="ffn_sample",
    )(x, g, w_in, w_dw, b_dw, w_out, g_final, hist2)
    return y, nh.reshape(m, 2, ffn_dim)


def _layernorm_silu(d, g, b):
    dc = d - jnp.mean(d, axis=-1, keepdims=True)
    y = dc * lax.rsqrt(jnp.mean(dc * dc, axis=-1, keepdims=True) + EPS) * g + b
    return _silu(y)


def _conf_prompt_body(x_ref, g_ref, w1_ref, b1_ref, wdw_ref, bdw_ref, lng_ref, lnb_ref, w2_ref, b2_ref,
                      o_ref, hist_ref, ext_ref, ph_ref, *, tm, ch, cw, pad):
    t = pl.program_id(1)

    @pl.when(t == 0)
    def _():
        ext_ref[0:pad, :] = jnp.zeros((pad, ch), F32)

    x = x_ref[...]
    h = _rmsnorm(x, g_ref[...]).astype(BF16)
    a = _dot(h, w1_ref[:, :ch]) + b1_ref[:, :ch]
    gt = _dot(h, w1_ref[:, ch:]) + b1_ref[:, ch:]
    ext_ref[pad:, :] = a * jax.nn.sigmoid(gt)
    span = ph_ref.shape[1]
    for s in range(1, SUBLANES):
        ph_ref[s - 1] = ext_ref[s:s + span, :]
    off = pad - (cw - 1)
    d = bdw_ref[...]
    for j in range(cw):
        base, s = (off + j) // SUBLANES * SUBLANES, (off + j) % SUBLANES
        src = ext_ref[base:base + tm, :] if s == 0 else ph_ref[s - 1, base:base + tm, :]
        d = d + wdw_ref[j:j + 1, :] * src
    y = _layernorm_silu(d, lng_ref[...], lnb_ref[...]).astype(BF16)
    o_ref[...] = x + _dot(y, w2_ref[...]) + b2_ref[...]

    @pl.when(t == pl.num_programs(1) - 1)
    def _():
        hist_ref[0] = ext_ref[tm + off:tm + pad, :]

    ext_ref[0:pad, :] = ext_ref[tm:tm + pad, :]


def _conf_prompt(x, g, w1, b1, wdw, bdw, lng, lnb, w2, b2, *, batch, seq_len, tm):
    m, d = x.shape
    cw, ch = wdw.shape
    pad = -(-(cw - 1) // SUBLANES) * SUBLANES
    nt = seq_len // tm
    row = pl.BlockSpec((tm, d), lambda b, t: (b * nt + t, 0))
    return pl.pallas_call(
        functools.partial(_conf_prompt_body, tm=tm, ch=ch, cw=cw, pad=pad),
        grid=(batch, nt),
        in_specs=[row, _full((1, d)), _full(w1.shape), _full((1, 2 * ch)), _full(wdw.shape), _full((1, ch)),
                  _full((1, ch)), _full((1, ch)), _full(w2.shape), _full((1, d))],
        out_specs=[row, pl.BlockSpec((1, cw - 1, ch), lambda b, t: (b, 0, 0))],
        out_shape=[jax.ShapeDtypeStruct((m, d), F32), jax.ShapeDtypeStruct((batch, cw - 1, ch), F32)],
        scratch_shapes=[pltpu.VMEM((tm + pad, ch), F32),
                        pltpu.VMEM((SUBLANES - 1, tm + pad - SUBLANES, ch), F32)],
        compiler_params=_cparams(("arbitrary", "arbitrary")),
        name="conf_prompt",
    )(x, g, w1, b1, wdw, bdw, lng, lnb, w2, b2)


def _conf_sample_body(x_ref, g_ref, w1_ref, b1_ref, wdw_ref, bdw_ref, lng_ref, lnb_ref, w2_ref, b2_ref,
                      hist_ref, o_ref, nh_ref, *, ch, cw):
    x = x_ref[...]
    h = _rmsnorm(x, g_ref[...]).astype(BF16)
    a = _dot(h, w1_ref[:, :ch]) + b1_ref[:, :ch]
    gt = _dot(h, w1_ref[:, ch:]) + b1_ref[:, ch:]
    u = a * jax.nn.sigmoid(gt)
    d = bdw_ref[...] + wdw_ref[cw - 1:cw, :] * u
    for j in range(cw - 1):
        d = d + wdw_ref[j:j + 1, :] * hist_ref[j]
    nh_ref[0:cw - 2] = hist_ref[1:cw - 1]
    nh_ref[cw - 2] = u
    y = _layernorm_silu(d, lng_ref[...], lnb_ref[...]).astype(BF16)
    o_ref[...] = x + _dot(y, w2_ref[...]) + b2_ref[...]


def _conf_sample(x, g, w1, b1, wdw, bdw, lng, lnb, w2, b2, hist_t, *, bb):
    m, d = x.shape
    cw, ch = wdw.shape
    row = lambda n: pl.BlockSpec((bb, n), lambda i: (i, 0))
    hs = pl.BlockSpec((cw - 1, bb, ch), lambda i: (0, i, 0))
    return pl.pallas_call(
        functools.partial(_conf_sample_body, ch=ch, cw=cw),
        grid=(m // bb,),
        in_specs=[row(d), _full((1, d)), _full(w1.shape), _full((1, 2 * ch)), _full(wdw.shape), _full((1, ch)),
                  _full((1, ch)), _full((1, ch)), _full(w2.shape), _full((1, d)), hs],
        out_specs=[row(d), hs],
        out_shape=[jax.ShapeDtypeStruct((m, d), F32), jax.ShapeDtypeStruct(hist_t.shape, F32)],
        compiler_params=_cparams(("arbitrary",)),
        name="conf_sample",
    )(x, g, w1, b1, wdw, bdw, lng, lnb, w2, b2, hist_t)


def kernel(x_prompt, x_sample, cache_fox_k, cache_fox_v, cache_fox_logf, page_table, state_hgrn, state_conv,
           state_ffn_conv, norm_mix, norm_ffn, norm_final, w_in0, fox_fb, hg_lb, hg_gnorm, w_out0, w_pw1, b_pw1,
           w_dw, b_dw, ln_g, ln_b, w_pw2, b_pw2, w_ffn_in, w_ffn_dw, b_ffn_dw, w_ffn_out):
    bp, seq_len, d = x_prompt.shape
    bs, dec_seq, _ = x_sample.shape
    assert dec_seq == 1, "the sample path handles one new token per sequence"
    depth = norm_mix.shape[0]
    n_fox_heads = fox_fb.shape[1]
    fox_w = n_fox_heads * FOX_DIM
    hg_w = hg_gnorm.shape[1]
    hg_heads = hg_w // HG_DIM
    assert hg_w == fox_w, "the combined projection is split into equal-width column blocks"
    row2 = lambda a: a.reshape(1, -1)

    xp = x_prompt.reshape(bp * seq_len, d)
    xs = x_sample.reshape(bs, d)
    tm = _tile(seq_len, 512)

    fk_p, fv_p, fl_p, fk_s, fv_s, fl_s = [], [], [], [], [], []
    hs_p, hs_s, cs_p, cs_s, ffs_p, ffs_s = [], [], [], [], [], []
    for l in range(depth):
        if l % 2 == 0:
            e = l // 2
            w_main = w_in0[e][:, :7 * hg_w].astype(BF16)
            w_qkvt = w_in0[e][:, 4 * hg_w:7 * hg_w].T.astype(BF16)
            w_ff = w_in0[e][:, 7 * hg_w:].astype(BF16)
            w_out = w_out0[e].astype(BF16)
            gn = row2(hg_gnorm[e])
            fb = row2(fox_fb[e])
            q, fr, v, gate, fqt, fk, fkt, fvt, flft, ct, caug = _inproj_prompt(
                xp, row2(norm_mix[l]), w_main, w_qkvt, w_ff, fb, hg_lb, layer_e=e, batch=bp, seq_len=seq_len, tm=tm)
            o_hg, s_fin = _hgrn_prompt(q, fr, v, gate, gn, batch=bp, seq_len=seq_len, tc=tm)
            o_fox = _fox_prompt(fqt, fk, caug, fvt, ct, batch=bp, seq_len=seq_len, tq=tm)
            xp = _outproj(xp, o_hg, o_fox, w_out, tm=tm)
            to_heads = lambda a: a.reshape(bp, n_fox_heads, FOX_DIM, seq_len).transpose(0, 3, 1, 2)
            fk_p.append(to_heads(fkt))
            fv_p.append(to_heads(fvt))
            fl_p.append(flft.transpose(0, 2, 1))
            hs_p.append(s_fin)
            q, fr, v, gate, fq, fk, fv, flf = _inproj_sample(
                xs, row2(norm_mix[l]), w_main, w_ff, fb, hg_lb, layer_e=e)
            o_hg, s_fin = _hgrn_sample(q, fr, v, gate, gn, state_hgrn[e], bb=8)
            o_fox = _fox_sample(page_table, cache_fox_k.transpose(0, 1, 3, 4, 2),
                                cache_fox_v.transpose(0, 1, 3, 4, 2), cache_fox_logf.transpose(0, 1, 3, 2),
                                fq, fk, fv, flf, layer_e=e)
            xs = _outproj(xs, o_hg, o_fox, w_out, tm=bs)
            fk_s.append(fk.reshape(bs, 1, n_fox_heads, FOX_DIM))
            fv_s.append(fv.reshape(bs, 1, n_fox_heads, FOX_DIM))
            fl_s.append(flf.reshape(bs, 1, n_fox_heads))
            hs_s.append(s_fin)
        else:
            o = l // 2
            args = (row2(norm_mix[l]), w_pw1[o].astype(BF16), row2(b_pw1[o]), w_dw[o], row2(b_dw[o]),
                    row2(ln_g[o]), row2(ln_b[o]), w_pw2[o].astype(BF16), row2(b_pw2[o]))
            xp, hist = _conf_prompt(xp, *args, batch=bp, seq_len=seq_len, tm=_tile(seq_len, 256))
            cs_p.append(hist)
            xs, hist_t = _conf_sample(xs, *args, state_conv[o].transpose(1, 0, 2), bb=32)
            cs_s.append(hist_t.transpose(1, 0, 2))
        last = l == depth - 1
        fargs = (row2(norm_ffn[l]), w_ffn_in[l].astype(BF16), w_ffn_dw[l], row2(b_ffn_dw[l]),
                 w_ffn_out[l].astype(BF16), row2(norm_final))
        xp, hist = _ffn_prompt(xp, *fargs, batch=bp, seq_len=seq_len, tm=tm, final_norm=last)
        ffs_p.append(hist)
        xs, hist = _ffn_sample(xs, *fargs, state_ffn_conv[l], final_norm=last)
        ffs_s.append(hist)

    return (xp.reshape(bp, seq_len, d), xs.reshape(bs, 1, d),
            jnp.stack(fk_p), jnp.stack(fv_p), jnp.stack(fl_p),
            jnp.stack(fk_s), jnp.stack(fv_s), jnp.stack(fl_s),
            jnp.stack(hs_p), jnp.stack(hs_s), jnp.stack(cs_p), jnp.stack(cs_s),
            jnp.stack(ffs_p), jnp.stack(ffs_s))
```

```python
import functools
import math

import numpy as np
import jax
import jax.numpy as jnp
from jax import lax
from jax.experimental import pallas as pl
from jax.experimental.pallas import tpu as pltpu

F32 = jnp.float32
BF16 = jnp.bfloat16
EPS = 1e-6
HG_DIM = 128
HG_CHUNK = 64
FOX_DIM = 64
LANES = 128
SUBLANES = 8
VMEM_LIMIT = 56 * 1024 * 1024
HIGHEST = lax.Precision.HIGHEST
LOG2E = math.log2(math.e)
BF16_ROWS = 16


def _cparams(sem):
    return pltpu.CompilerParams(dimension_semantics=sem, vmem_limit_bytes=VMEM_LIMIT)


def _tile(n, target):
    t = min(n, target)
    while n % t:
        t -= 8
    return t


def _rmsnorm(x, g):
    return x * lax.rsqrt(jnp.mean(x * x, axis=-1, keepdims=True) + EPS) * g


def _silu(x):
    return x * jax.nn.sigmoid(x)


def _log_sigmoid(x):
    return -(jnp.maximum(-x, 0.0) + jnp.log1p(jnp.exp(-jnp.abs(x))))


def _dot(a, b):
    return jnp.dot(a, b, preferred_element_type=F32)


def _dot_nt(a, b):
    return lax.dot_general(a, b, (((1,), (1,)), ((), ())), preferred_element_type=F32)


def _full(shape):
    return pl.BlockSpec(shape, lambda *_: (0,) * len(shape))


def _inproj_common(h, w_ref, lb_ref, q_ref, fr_ref, v_ref, gate_ref, *, layer_e, hgw):
    def proj(k):
        return _dot(h, w_ref[:, k * hgw:(k + 1) * hgw])

    lbp = lb_ref[...]
    ex = jnp.exp(lbp - jnp.max(lbp, axis=0, keepdims=True))
    sm = ex / jnp.sum(ex, axis=0, keepdims=True)
    lb = jnp.sum(sm[:layer_e + 1], axis=0, keepdims=True)

    q_ref[...] = _silu(proj(0))
    fr_ref[...] = lb + (1.0 - lb) * jax.nn.sigmoid(proj(1))
    v_ref[...] = proj(2)
    gate_ref[...] = _silu(proj(3))
    return proj


def _split3(x):
    hi = x.astype(BF16)
    r = x - hi.astype(F32)
    mid = r.astype(BF16)
    return hi, mid, (r - mid.astype(F32)).astype(BF16)


def _inproj_prompt_body(x_ref, g_ref, w_ref, wqkvt_ref, wff_ref, wfft_ref, fb_ref, fbt_ref, lb_ref, tri_ref,
                        trit_ref, place_ref, q_ref, fr_ref, v_ref, gate_ref, fqt_ref, fk_ref, fkt_ref, fvt_ref,
                        flft_ref, ct_ref, caug_ref, carry_ref, carryt_ref, *, layer_e, hgw):
    h = _rmsnorm(x_ref[...], g_ref[...]).astype(BF16)

    @pl.when(pl.program_id(1) == 0)
    def _():
        carry_ref[...] = jnp.zeros_like(carry_ref)
        carryt_ref[...] = jnp.zeros_like(carryt_ref)

    flf = _log_sigmoid(_dot(h, wff_ref[...]) + fb_ref[...])
    c = sum(_dot(tri_ref[...], p) for p in _split3(flf)) + carry_ref[...]
    carry_ref[...] = c[-1:, :]
    parts = _split3(c * LOG2E)
    caug_ref[...] = sum(_dot(p, place_ref[i]) for i, p in enumerate(parts)).astype(BF16)
    flft = _log_sigmoid(_dot_nt(wfft_ref[...], h) + fbt_ref[...])
    flft_ref[0] = flft
    ct = sum(_dot(p, trit_ref[...]) for p in _split3(flft)) + carryt_ref[...]
    ct_ref[0] = ct
    carryt_ref[...] = ct[:, -1:]

    proj = _inproj_common(h, w_ref, lb_ref, q_ref, fr_ref, v_ref, gate_ref, layer_e=layer_e, hgw=hgw)
    fk_ref[...] = proj(5)
    qkvt = _dot_nt(wqkvt_ref[...], h)
    fqt_ref[0] = qkvt[0:hgw]
    fkt_ref[0] = qkvt[hgw:2 * hgw]
    fvt_ref[0] = qkvt[2 * hgw:]


def _caug_lane(head, part):
    return LANES * (head // 2) + 3 * (head % 2) + part


def _inproj_prompt(x, g, w_main, w_qkvt, w_ff, fb, hg_lb, *, layer_e, batch, seq_len, tm):
    m, d = x.shape
    hgw = w_qkvt.shape[0] // 3
    nfh = w_ff.shape[1]
    nt = seq_len // tm
    tri = np.tril(np.ones((tm, tm), np.float32))
    place = np.zeros((3, nfh, hgw), np.float32)
    for part in range(3):
        for head in range(nfh):
            place[part, head, _caug_lane(head, part)] = 1.0
    row = lambda n: pl.BlockSpec((tm, n), lambda b, t: (b * nt + t, 0))
    tr = lambda n: pl.BlockSpec((1, n, tm), lambda b, t: (b, 0, t))
    wide = jax.ShapeDtypeStruct((m, hgw), F32)
    wide_t = jax.ShapeDtypeStruct((batch, hgw, seq_len), F32)
    narrow_t = jax.ShapeDtypeStruct((batch, nfh, seq_len), F32)
    return pl.pallas_call(
        functools.partial(_inproj_prompt_body, layer_e=layer_e, hgw=hgw),
        grid=(batch, nt),
        in_specs=[row(d), _full((1, d)), _full(w_main.shape), _full(w_qkvt.shape), _full(w_ff.shape),
                  _full((nfh, d)), _full((1, nfh)), _full((nfh, 1)), _full(hg_lb.shape), _full((tm, tm)),
                  _full((tm, tm)), _full(place.shape)],
        out_specs=[row(hgw)] * 4 + [tr(hgw), row(hgw), tr(hgw), tr(hgw), tr(nfh), tr(nfh), row(hgw)],
        out_shape=[wide] * 4 + [wide_t, wide, wide_t, wide_t, narrow_t, narrow_t,
                                jax.ShapeDtypeStruct((m, hgw), BF16)],
        scratch_shapes=[pltpu.VMEM((1, nfh), F32), pltpu.VMEM((nfh, 1), F32)],
        compiler_params=_cparams(("arbitrary", "arbitrary")),
        name="inproj_prompt",
    )(x, g, w_main, w_qkvt, w_ff, w_ff.T, fb, fb.reshape(nfh, 1), hg_lb, jnp.asarray(tri, BF16), jnp.asarray(tri.T, BF16),
      jnp.asarray(place, BF16))


def _inproj_sample_body(x_ref, g_ref, w_ref, wff_ref, fb_ref, lb_ref,
                        q_ref, fr_ref, v_ref, gate_ref, fq_ref, fk_ref, fv_ref, flf_ref, *, layer_e, hgw):
    h = _rmsnorm(x_ref[...], g_ref[...]).astype(BF16)
    proj = _inproj_common(h, w_ref, lb_ref, q_ref, fr_ref, v_ref, gate_ref, layer_e=layer_e, hgw=hgw)
    fq_ref[...] = proj(4)
    fk_ref[...] = proj(5)
    fv_ref[...] = proj(6)
    flf_ref[...] = _log_sigmoid(_dot(h, wff_ref[...]) + fb_ref[...])


def _inproj_sample(x, g, w_main, w_ff, fb, hg_lb, *, layer_e):
    m, d = x.shape
    hgw = w_main.shape[1] // 7
    nfh = w_ff.shape[1]
    wide = jax.ShapeDtypeStruct((m, hgw), F32)
    return pl.pallas_call(
        functools.partial(_inproj_sample_body, layer_e=layer_e, hgw=hgw),
        grid=(1,),
        in_specs=[_full((m, d)), _full((1, d)), _full(w_main.shape), _full(w_ff.shape), _full((1, nfh)),
                  _full(hg_lb.shape)],
        out_specs=[_full((m, hgw))] * 7 + [_full((m, nfh))],
        out_shape=[wide] * 7 + [jax.ShapeDtypeStruct((m, nfh), F32)],
        compiler_params=_cparams(("arbitrary",)),
        name="inproj_sample",
    )(x, g, w_main, w_ff, fb, hg_lb)


def _hgrn_prompt_body(q_ref, fr_ref, v_ref, gate_ref, gn_ref, tri_ref, o_ref, s_ref,
                      st_ref, oi_ref, qb_ref, kvt_ref, dec_ref, *, n_chunks, n_heads, group):
    t = pl.program_id(1)

    @pl.when(t == 0)
    def _():
        st_ref[...] = jnp.zeros_like(st_ref)

    tri = tri_ref[...]
    rows = lax.broadcasted_iota(jnp.int32, (HG_CHUNK, HG_CHUNK), 0)
    cols = lax.broadcasted_iota(jnp.int32, (HG_CHUNK, HG_CHUNK), 1)
    causal = rows >= cols
    gn = gn_ref[...]

    heads = [slice(h * HG_DIM, (h + 1) * HG_DIM) for h in range(n_heads)]

    def local(g, carry):
        cs = [g * group + i for i in range(group)]
        rows_of = [pl.ds(pl.multiple_of(c * HG_CHUNK, HG_CHUNK), HG_CHUNK) for c in cs]
        frs = [fr_ref[rs, :] for rs in rows_of]
        splits = [_split3(jnp.log(fr)) for fr in frs]
        bs = [sum(_dot(tri, p) for p in parts) for parts in splits]
        qrs, krs, kdecs, vs = [], [], [], []
        for c, rs, fr, b in zip(cs, rows_of, frs, bs):
            q = q_ref[rs, :]
            k = 1.0 - fr
            b_last = b[HG_CHUNK - 1:HG_CHUNK, :]
            b_mid = b[HG_CHUNK // 2:HG_CHUNK // 2 + 1, :]
            kdecs.append((k * jnp.exp(b_last - b)).astype(BF16))
            qrs.append((q * jnp.exp(b - b_mid)).astype(BF16))
            krs.append((k * jnp.exp(b_mid - b)).astype(BF16))
            vs.append(v_ref[rs, :])
            qb_ref[c] = (q * jnp.exp(b)).astype(BF16)
            dec_ref[c] = jnp.exp(b_last)
        atts = [[_dot_nt(qr[:, sl], kr[:, sl]) for sl in heads] for qr, kr in zip(qrs, krs)]
        for c, v, kdec in zip(cs, vs, kdecs):
            for h, sl in enumerate(heads):
                kvt_ref[c, h] = _dot(v[:, sl].T.astype(BF16), kdec[:, sl])
        for c, v, att in zip(cs, vs, atts):
            for h, sl in enumerate(heads):
                oi_ref[c, :, sl] = _dot(jnp.where(causal, att[h], 0.0).astype(BF16), v[:, sl].astype(BF16))
        return carry

    def carried(c, carry):
        rs = pl.ds(pl.multiple_of(c * HG_CHUNK, HG_CHUNK), HG_CHUNK)
        gate = gate_ref[rs, :]
        qb = qb_ref[c]
        dec = dec_ref[c]
        for h in range(n_heads):
            sl = slice(h * HG_DIM, (h + 1) * HG_DIM)
            st = st_ref[h]
            o = oi_ref[c, :, sl] + _dot_nt(qb[:, sl], st.astype(BF16))
            st_ref[h] = st * dec[:, sl] + kvt_ref[c, h]
            o = o * lax.rsqrt(jnp.mean(o * o, axis=-1, keepdims=True) + EPS)
            o_ref[rs, sl] = (o * gn[:, sl] * gate[:, sl]).astype(o_ref.dtype)
        return carry

    lax.fori_loop(0, n_chunks // group, local, 0)
    lax.fori_loop(0, n_chunks, carried, 0, unroll=2)

    @pl.when(t == pl.num_programs(1) - 1)
    def _():
        for h in range(n_heads):
            s_ref[0, h] = st_ref[h].T


def _hgrn_prompt(q, fr, v, gate, gn, *, batch, seq_len, tc):
    m, hgw = q.shape
    n_heads = hgw // HG_DIM
    nt = seq_len // tc
    tri = jnp.asarray(np.tril(np.ones((HG_CHUNK, HG_CHUNK), np.float32)), BF16)
    n_chunks = tc // HG_CHUNK
    row = pl.BlockSpec((tc, hgw), lambda b, t: (b * nt + t, 0))
    return pl.pallas_call(
        functools.partial(_hgrn_prompt_body, n_chunks=n_chunks, n_heads=n_heads,
                          group=math.gcd(n_chunks, 4)),
        grid=(batch, nt),
        in_specs=[row, row, row, row, _full((1, hgw)), _full((HG_CHUNK, HG_CHUNK))],
        out_specs=[row, pl.BlockSpec((1, n_heads, HG_DIM, HG_DIM), lambda b, t: (b, 0, 0, 0))],
        out_shape=[jax.ShapeDtypeStruct((m, hgw), BF16),
                   jax.ShapeDtypeStruct((batch, n_heads, HG_DIM, HG_DIM), F32)],
        scratch_shapes=[pltpu.VMEM((n_heads, HG_DIM, HG_DIM), F32),
                        pltpu.VMEM((n_chunks, HG_CHUNK, hgw), F32),
                        pltpu.VMEM((n_chunks, HG_CHUNK, hgw), BF16),
                        pltpu.VMEM((n_chunks, n_heads, HG_DIM, HG_DIM), F32),
                        pltpu.VMEM((n_chunks, 1, hgw), F32)],
        compiler_params=_cparams(("arbitrary", "arbitrary")),
        name="hgrn_prompt",
    )(q, fr, v, gate, gn, tri)


def _hgrn_sample_body(q_ref, fr_ref, v_ref, gate_ref, gn_ref, s_ref, o_ref, so_ref, *, bb, n_heads):
    def col(r):
        return jnp.broadcast_to(r, (HG_DIM, HG_DIM)).T

    for i in range(bb):
        for h in range(n_heads):
            sl = slice(h * HG_DIM, (h + 1) * HG_DIM)
            fr = fr_ref[i:i + 1, sl]
            dec = jnp.exp(jnp.log(fr))
            s_new = col(dec) * s_ref[i, h] + col(1.0 - fr) * v_ref[i:i + 1, sl]
            so_ref[i, h] = s_new
            o = jnp.sum(col(q_ref[i:i + 1, sl]) * s_new, axis=0, keepdims=True)
            o = o * lax.rsqrt(jnp.mean(o * o, axis=-1, keepdims=True) + EPS)
            o_ref[i:i + 1, sl] = (o * gn_ref[:, sl] * gate_ref[i:i + 1, sl]).astype(o_ref.dtype)


def _hgrn_sample(q, fr, v, gate, gn, state, *, bb):
    m, hgw = q.shape
    n_heads = hgw // HG_DIM
    row = pl.BlockSpec((bb, hgw), lambda i: (i, 0))
    st = pl.BlockSpec((bb, n_heads, HG_DIM, HG_DIM), lambda i: (i, 0, 0, 0))
    return pl.pallas_call(
        functools.partial(_hgrn_sample_body, bb=bb, n_heads=n_heads),
        grid=(m // bb,),
        in_specs=[row, row, row, row, _full((1, hgw)), st],
        out_specs=[row, st],
        out_shape=[jax.ShapeDtypeStruct((m, hgw), BF16), jax.ShapeDtypeStruct(state.shape, F32)],
        compiler_params=_cparams(("arbitrary",)),
        name="hgrn_sample",
    )(q, fr, v, gate, gn, state)


def _fox_prompt_body(qi_ref, ki_ref, qt_ref, k_ref, ca_ref, vt_ref, ct_ref, o_ref, m_ref, acc_ref, *, tk):
    j = pl.program_id(2)
    qi = qi_ref[j]
    ki = ki_ref[j]
    tq = 2 * tk

    @pl.when(ki == 0)
    def _():
        m_ref[...] = jnp.full_like(m_ref, -jnp.inf)
        acc_ref[...] = jnp.zeros_like(acc_ref)

    sub = lax.broadcasted_iota(jnp.int32, (LANES, 1), 0)

    def update(units, masked):
        qt = qt_ref[0] * (FOX_DIM ** -0.5 * LOG2E)
        kaug = jnp.concatenate([k_ref[0].astype(BF16), ca_ref[0]], axis=1)
        vaug = jnp.concatenate([vt_ref[0].astype(BF16), jnp.ones((BF16_ROWS, tk), BF16)], axis=0)
        keep = (lax.broadcasted_iota(jnp.int32, (tk, tk), 1)
                >= lax.broadcasted_iota(jnp.int32, (tk, tk), 0))
        lanes = lambda half: slice(half * tk, (half + 1) * tk)
        m_prev = {u: m_ref[u[0], :, lanes(u[1])] for u in units}
        acc_prev = {u: acc_ref[u[0], :, lanes(u[1])] for u in units}
        qaug = {}
        for hh in sorted({u[0] for u in units}):
            qm = jnp.where(sub // FOX_DIM == hh, qt, 0.0)
            neg = jnp.where((sub >= 3 * hh) & (sub < 3 * hh + 3), -1.0, 0.0)
            qaug[hh] = jnp.concatenate([qm, jnp.broadcast_to(neg, (LANES, tq))], axis=0).astype(BF16)
        s_all = {u: _dot(kaug, qaug[u[0]][:, lanes(u[1])]) for u in units}
        m_out, acc_out = {}, {}
        for u in units:
            hh, half = u
            s = jnp.where(keep, s_all[u], -jnp.inf) if u in masked else s_all[u]
            ct = ct_ref[0, 0, hh:hh + 1, lanes(half)] * LOG2E
            m_new = jnp.maximum(m_prev[u], jnp.max(s, axis=0, keepdims=True) + ct)
            p = jnp.exp2(s - (m_new - ct))
            alpha = jnp.exp2(m_prev[u] - m_new)
            acc_out[u] = alpha * acc_prev[u] + _dot(vaug, p.astype(BF16))
            m_out[u] = m_new
        for u in units:
            acc_ref[u[0], :, lanes(u[1])] = acc_out[u]
            m_ref[u[0], :, lanes(u[1])] = m_out[u]

    every = ((0, 0), (1, 0), (0, 1), (1, 1))
    upper = ((0, 1), (1, 1))

    @pl.when(ki < 2 * qi)
    def _():
        update(every, ())

    @pl.when(ki == 2 * qi)
    def _():
        update(every, ((0, 0), (1, 0)))

    @pl.when(ki == 2 * qi + 1)
    def _():
        update(upper, upper)
        a0 = acc_ref[0]
        a1 = acc_ref[1]
        o0 = a0[:LANES] * (1.0 / a0[LANES:LANES + 1])
        o1 = a1[:LANES] * (1.0 / a1[LANES:LANES + 1])
        o_ref[0] = jnp.where(sub // FOX_DIM == 0, o0, o1).T.astype(o_ref.dtype)


def _fox_prompt(fqt, fk, caug, fvt, ct, *, batch, seq_len, tk):
    m, fw = fk.shape
    n_pairs = fw // LANES
    tq = 2 * tk
    nq = seq_len // tq
    qi_tab = np.array([q for q in range(nq) for _ in range(2 * q + 2)], np.int32)
    ki_tab = np.array([k for q in range(nq) for k in range(2 * q + 2)], np.int32)
    r3 = lambda a: a.reshape(batch, seq_len, fw)
    by_q = lambda b, p, j, qt, kt: (b, p, qt[j])
    by_k = lambda b, p, j, qt, kt: (b, kt[j], p)
    grid_spec = pltpu.PrefetchScalarGridSpec(
        num_scalar_prefetch=2,
        grid=(batch, n_pairs, len(qi_tab)),
        in_specs=[
            pl.BlockSpec((1, LANES, tq), by_q),
            pl.BlockSpec((1, tk, LANES), by_k),
            pl.BlockSpec((1, tk, LANES), by_k),
            pl.BlockSpec((1, LANES, tk), lambda b, p, j, qt, kt: (b, p, kt[j])),
            pl.BlockSpec((1, 1, 2, tq), lambda b, p, j, qt, kt: (b, p, 0, qt[j])),
        ],
        out_specs=pl.BlockSpec((1, tq, LANES), lambda b, p, j, qt, kt: (b, qt[j], p)),
        scratch_shapes=[pltpu.VMEM((2, 1, tq), F32), pltpu.VMEM((2, LANES + BF16_ROWS, tq), F32)],
    )
    o = pl.pallas_call(
        functools.partial(_fox_prompt_body, tk=tk),
        grid_spec=grid_spec,
        out_shape=jax.ShapeDtypeStruct((batch, seq_len, fw), BF16),
        compiler_params=_cparams(("arbitrary", "arbitrary", "arbitrary")),
        name="fox_prompt",
    )(jnp.asarray(qi_tab), jnp.asarray(ki_tab), fqt, r3(fk), r3(caug), fvt,
      ct.reshape(batch, n_pairs, 2, seq_len))
    return o.reshape(m, fw)


def _fox_sample_body(pt_ref, *refs, n_pages, n_heads):
    k_refs, v_refs, lf_refs = refs[:n_pages], refs[n_pages:2 * n_pages], refs[2 * n_pages:3 * n_pages]
    q_ref, kn_ref, vn_ref, cn_ref, o_ref = refs[3 * n_pages:]
    page = lf_refs[0].shape[-1]
    width = n_heads * FOX_DIM
    rows = n_pages * n_heads
    scale = FOX_DIM ** -0.5

    diag = (lax.broadcasted_iota(jnp.int32, (n_heads, width), 1) // FOX_DIM
            == lax.broadcasted_iota(jnp.int32, (n_heads, width), 0))
    qblk = jnp.where(diag, q_ref[0] * scale, 0.0)
    s_new = jnp.sum(qblk * kn_ref[0], axis=-1, keepdims=True)

    lf = jnp.concatenate([r[0, 0] for r in lf_refs], axis=0)
    pj = lax.broadcasted_iota(jnp.int32, (page, page), 0)
    ps = lax.broadcasted_iota(jnp.int32, (page, page), 1)
    within = jnp.dot(lf, jnp.where(pj > ps, 1.0, 0.0).astype(F32), precision=HIGHEST,
                     preferred_element_type=F32)
    rr = lax.broadcasted_iota(jnp.int32, (rows, rows), 0)
    rc = lax.broadcasted_iota(jnp.int32, (rows, rows), 1)
    later_pages = jnp.where(rc % n_heads == rr % n_heads, jnp.where(rc // n_heads > rr // n_heads, 1.0, 0.0), 0.0)
    tot = jnp.broadcast_to(jnp.sum(lf, axis=-1, keepdims=True), (rows, page))
    suffix = within + jnp.dot(later_pages.astype(F32), tot, precision=HIGHEST, preferred_element_type=F32)

    qb = qblk.astype(BF16)
    cn = cn_ref[0]
    s_list = []
    for g in range(n_pages):
        kt = k_refs[g][0, 0].reshape(width, page).astype(BF16)
        s_list.append(_dot(qb, kt) + cn + suffix[g * n_heads:(g + 1) * n_heads])
    m = s_list[0]
    for s in s_list[1:]:
        m = jnp.maximum(m, s)
    m = jnp.maximum(jnp.max(m, axis=-1, keepdims=True), s_new)
    p_new = jnp.exp(s_new - m)
    l = p_new
    acc = p_new * vn_ref[0]
    for g in range(n_pages):
        p = jnp.exp(s_list[g] - m)
        l = l + jnp.sum(p, axis=-1, keepdims=True)
        vt = v_refs[g][0, 0].reshape(width, page).astype(BF16)
        acc = acc + _dot_nt(p.astype(BF16), vt)
    o_ref[0] = jnp.sum(jnp.where(diag, acc / l, 0.0), axis=0, keepdims=True)


def _fox_sample(page_table, cache_kt, cache_vt, cache_lft, fq, fk, fv, flf, *, layer_e):
    bs, n_pages = page_table.shape
    _, _, n_heads, _, page = cache_kt.shape
    width = n_heads * FOX_DIM

    def kv_spec(g):
        return pl.BlockSpec((1, 1, n_heads, FOX_DIM, page),
                            lambda b, pt: (layer_e, pt[b * n_pages + g], 0, 0, 0))

    def lf_spec(g):
        return pl.BlockSpec((1, 1, n_heads, page), lambda b, pt: (layer_e, pt[b * n_pages + g], 0, 0))

    tok = pl.BlockSpec((1, 1, width), lambda b, pt: (b, 0, 0))
    r3 = lambda a: a.reshape(bs, 1, width)
    grid_spec = pltpu.PrefetchScalarGridSpec(
        num_scalar_prefetch=1, grid=(bs,),
        in_specs=[kv_spec(g) for g in range(n_pages)] * 2 + [lf_spec(g) for g in range(n_pages)] + [
            tok, tok, tok, pl.BlockSpec((1, n_heads, 1), lambda b, pt: (b, 0, 0))],
        out_specs=tok)
    o = pl.pallas_call(
        functools.partial(_fox_sample_body, n_pages=n_pages, n_heads=n_heads),
        grid_spec=grid_spec,
        out_shape=jax.ShapeDtypeStruct((bs, 1, width), F32),
        compiler_params=_cparams(("arbitrary",)),
        name="fox_sample",
    )(page_table.reshape(-1), *([cache_kt] * n_pages), *([cache_vt] * n_pages), *([cache_lft] * n_pages),
      r3(fq), r3(fk), r3(fv), flf.reshape(bs, n_heads, 1))
    return o.reshape(bs, width)


def _outproj_body(x_ref, a_ref, b_ref, w_ref, o_ref):
    ka = a_ref.shape[1]
    o_ref[...] = (x_ref[...] + _dot(a_ref[...].astype(BF16), w_ref[:ka, :])
                  + _dot(b_ref[...].astype(BF16), w_ref[ka:, :]))


def _outproj(x, a, b, w, *, tm):
    m, d = x.shape
    row = lambda n: pl.BlockSpec((tm, n), lambda i: (i, 0))
    return pl.pallas_call(
        _outproj_body, grid=(m // tm,),
        in_specs=[row(d), row(a.shape[1]), row(b.shape[1]), _full(w.shape)],
        out_specs=row(d), out_shape=jax.ShapeDtypeStruct((m, d), F32),
        compiler_params=_cparams(("arbitrary",)), name="outproj",
    )(x, a, b, w)


def _gelu(x):
    return 0.5 * x * (1.0 + lax.erf(x * (2.0 ** -0.5)))


def _ffn_cols(ffn_dim):
    cw = 256
    while ffn_dim % cw:
        cw //= 2
    return cw


def _ffn_prompt_body(x_ref, g_ref, win_ref, wdw_ref, bdw_ref, wout_ref, gfin_ref,
                     o_ref, hist_ref, ext_ref, carry_ref, act_ref, *, tm, ffn_dim, cw, final_norm):
    t = pl.program_id(1)

    @pl.when(t == 0)
    def _():
        carry_ref[...] = jnp.zeros_like(carry_ref)

    x = x_ref[...]
    h = _rmsnorm(x, g_ref[...]).astype(BF16)
    for j in range(ffn_dim // cw):
        cs = slice(j * cw, (j + 1) * cw)
        g = _dot(h, win_ref[:, cs])
        u = _dot(h, win_ref[:, ffn_dim + j * cw:ffn_dim + (j + 1) * cw])
        ext_ref[6:8, :] = carry_ref[6:8, cs]
        ext_ref[8:, :] = g
        carry_ref[6:8, cs] = g[tm - 2:, :]
        gc = (wdw_ref[0:1, cs] * ext_ref[6:6 + tm, :] + wdw_ref[1:2, cs] * ext_ref[7:7 + tm, :]
              + wdw_ref[2:3, cs] * g + bdw_ref[:, cs])
        act_ref[:, cs] = (_gelu(gc) * u).astype(BF16)
    y = x + _dot(act_ref[...], wout_ref[...])
    if final_norm:
        y = _rmsnorm(y, gfin_ref[...])
    o_ref[...] = y

    @pl.when(t == pl.num_programs(1) - 1)
    def _():
        hist_ref[0] = carry_ref[6:8, :]


def _ffn_prompt(x, g, w_in, w_dw, b_dw, w_out, g_final, *, batch, seq_len, tm, final_norm):
    m, d = x.shape
    ffn_dim = w_out.shape[0]
    cw = _ffn_cols(ffn_dim)
    nt = seq_len // tm
    row = pl.BlockSpec((tm, d), lambda b, t: (b * nt + t, 0))
    return pl.pallas_call(
        functools.partial(_ffn_prompt_body, tm=tm, ffn_dim=ffn_dim, cw=cw, final_norm=final_norm),
        grid=(batch, nt),
        in_specs=[row, _full((1, d)), _full(w_in.shape), _full(w_dw.shape), _full((1, ffn_dim)),
                  _full(w_out.shape), _full((1, d))],
        out_specs=[row, pl.BlockSpec((1, 2, ffn_dim), lambda b, t: (b, 0, 0))],
        out_shape=[jax.ShapeDtypeStruct((m, d), F32), jax.ShapeDtypeStruct((batch, 2, ffn_dim), F32)],
        scratch_shapes=[pltpu.VMEM((tm + 8, cw), F32), pltpu.VMEM((8, ffn_dim), F32),
                        pltpu.VMEM((tm, ffn_dim), BF16)],
        compiler_params=_cparams(("arbitrary", "arbitrary")),
        name="ffn_prompt",
    )(x, g, w_in, w_dw, b_dw, w_out, g_final)


def _ffn_sample_body(x_ref, g_ref, win_ref, wdw_ref, bdw_ref, wout_ref, gfin_ref, hist_ref,
                     o_ref, nh_ref, act_ref, *, ffn_dim, cw, final_norm):
    x = x_ref[...]
    h = _rmsnorm(x, g_ref[...]).astype(BF16)
    for j in range(ffn_dim // cw):
        cs = slice(j * cw, (j + 1) * cw)
        g = _dot(h, win_ref[:, cs])
        u = _dot(h, win_ref[:, ffn_dim + j * cw:ffn_dim + (j + 1) * cw])
        h0 = hist_ref[:, cs]
        h1 = hist_ref[:, ffn_dim + j * cw:ffn_dim + (j + 1) * cw]
        gc = wdw_ref[0:1, cs] * h0 + wdw_ref[1:2, cs] * h1 + wdw_ref[2:3, cs] * g + bdw_ref[:, cs]
        act_ref[:, cs] = (_gelu(gc) * u).astype(BF16)
        nh_ref[:, cs] = h1
        nh_ref[:, ffn_dim + j * cw:ffn_dim + (j + 1) * cw] = g
    y = x + _dot(act_ref[...], wout_ref[...])
    if final_norm:
        y = _rmsnorm(y, gfin_ref[...])
    o_ref[...] = y


def _ffn_sample(x, g, w_in, w_dw, b_dw, w_out, g_final, hist, *, final_norm):
    m, d = x.shape
    ffn_dim = w_out.shape[0]
    cw = _ffn_cols(ffn_dim)
    hist2 = hist.reshape(m, 2 * ffn_dim)
    y, nh = pl.pallas_call(
        functools.partial(_ffn_sample_body, ffn_dim=ffn_dim, cw=cw, final_norm=final_norm),
        grid=(1,),
        in_specs=[_full((m, d)), _full((1, d)), _full(w_in.shape), _full(w_dw.shape),
                  _full((1, ffn_dim)), _full(w_out.shape), _full((1, d)), _full(hist2.shape)],
        out_specs=[_full((m, d)), _full(hist2.shape)],
        out_shape=[jax.ShapeDtypeStruct((m, d), F32), jax.ShapeDtypeStruct(hist2.shape, F32)],
        scratch_shapes=[pltpu.VMEM((m, ffn_dim), BF16)],
        compiler_params=_cparams(("arbitrary",)),
        name="ffn_sample",
    )(x, g, w_in, w_dw, b_dw, w_out, g_final, hist2)
    return y, nh.reshape(m, 2, ffn_dim)


def _layernorm_silu(d, g, b):
    dc = d - jnp.mean(d, axis=-1, keepdims=True)
    y = dc * lax.rsqrt(jnp.mean(dc * dc, axis=-1, keepdims=True) + EPS) * g + b
    return _silu(y)


def _conf_prompt_body(x_ref, g_ref, w1_ref, b1_ref, wdw_ref, bdw_ref, lng_ref, lnb_ref, w2_ref, b2_ref,
                      o_ref, hist_ref, ext_ref, ph_ref, *, tm, ch, cw, pad):
    t = pl.program_id(1)

    @pl.when(t == 0)
    def _():
        ext_ref[0:pad, :] = jnp.zeros((pad, ch), F32)

    x = x_ref[...]
    h = _rmsnorm(x, g_ref[...]).astype(BF16)
    a = _dot(h, w1_ref[:, :ch]) + b1_ref[:, :ch]
    gt = _dot(h, w1_ref[:, ch:]) + b1_ref[:, ch:]
    ext_ref[pad:, :] = a * jax.nn.sigmoid(gt)
    span = ph_ref.shape[1]
    for s in range(1, SUBLANES):
        ph_ref[s - 1] = ext_ref[s:s + span, :]
    off = pad - (cw - 1)
    d = bdw_ref[...]
    for j in range(cw):
        base, s = (off + j) // SUBLANES * SUBLANES, (off + j) % SUBLANES
        src = ext_ref[base:base + tm, :] if s == 0 else ph_ref[s - 1, base:base + tm, :]
        d = d + wdw_ref[j:j + 1, :] * src
    y = _layernorm_silu(d, lng_ref[...], lnb_ref[...]).astype(BF16)
    o_ref[...] = x + _dot(y, w2_ref[...]) + b2_ref[...]

    @pl.when(t == pl.num_programs(1) - 1)
    def _():
        hist_ref[0] = ext_ref[tm + off:tm + pad, :]

    ext_ref[0:pad, :] = ext_ref[tm:tm + pad, :]


def _conf_prompt(x, g, w1, b1, wdw, bdw, lng, lnb, w2, b2, *, batch, seq_len, tm):
    m, d = x.shape
    cw, ch = wdw.shape
    pad = -(-(cw - 1) // SUBLANES) * SUBLANES
    nt = seq_len // tm
    row = pl.BlockSpec((tm, d), lambda b, t: (b * nt + t, 0))
    return pl.pallas_call(
        functools.partial(_conf_prompt_body, tm=tm, ch=ch, cw=cw, pad=pad),
        grid=(batch, nt),
        in_specs=[row, _full((1, d)), _full(w1.shape), _full((1, 2 * ch)), _full(wdw.shape), _full((1, ch)),
                  _full((1, ch)), _full((1, ch)), _full(w2.shape), _full((1, d))],
        out_specs=[row, pl.BlockSpec((1, cw - 1, ch), lambda b, t: (b, 0, 0))],
        out_shape=[jax.ShapeDtypeStruct((m, d), F32), jax.ShapeDtypeStruct((batch, cw - 1, ch), F32)],
        scratch_shapes=[pltpu.VMEM((tm + pad, ch), F32),
                        pltpu.VMEM((SUBLANES - 1, tm + pad - SUBLANES, ch), F32)],
        compiler_params=_cparams(("arbitrary", "arbitrary")),
        name="conf_prompt",
    )(x, g, w1, b1, wdw, bdw, lng, lnb, w2, b2)


def _conf_sample_body(x_ref, g_ref, w1_ref, b1_ref, wdw_ref, bdw_ref, lng_ref, lnb_ref, w2_ref, b2_ref,
                      hist_ref, o_ref, nh_ref, *, ch, cw):
    x = x_ref[...]
    h = _rmsnorm(x, g_ref[...]).astype(BF16)
    a = _dot(h, w1_ref[:, :ch]) + b1_ref[:, :ch]
    gt = _dot(h, w1_ref[:, ch:]) + b1_ref[:, ch:]
    u = a * jax.nn.sigmoid(gt)
    d = bdw_ref[...] + wdw_ref[cw - 1:cw, :] * u
    for j in range(cw - 1):
        d = d + wdw_ref[j:j + 1, :] * hist_ref[j]
    nh_ref[0:cw - 2] = hist_ref[1:cw - 1]
    nh_ref[cw - 2] = u
    y = _layernorm_silu(d, lng_ref[...], lnb_ref[...]).astype(BF16)
    o_ref[...] = x + _dot(y, w2_ref[...]) + b2_ref[...]


def _conf_sample(x, g, w1, b1, wdw, bdw, lng, lnb, w2, b2, hist_t, *, bb):
    m, d = x.shape
    cw, ch = wdw.shape
    row = lambda n: pl.BlockSpec((bb, n), lambda i: (i, 0))
    hs = pl.BlockSpec((cw - 1, bb, ch), lambda i: (0, i, 0))
    return pl.pallas_call(
        functools.partial(_conf_sample_body, ch=ch, cw=cw),
        grid=(m // bb,),
        in_specs=[row(d), _full((1, d)), _full(w1.shape), _full((1, 2 * ch)), _full(wdw.shape), _full((1, ch)),
                  _full((1, ch)), _full((1, ch)), _full(w2.shape), _full((1, d)), hs],
        out_specs=[row(d), hs],
        out_shape=[jax.ShapeDtypeStruct((m, d), F32), jax.ShapeDtypeStruct(hist_t.shape, F32)],
        compiler_params=_cparams(("arbitrary",)),
        name="conf_sample",
    )(x, g, w1, b1, wdw, bdw, lng, lnb, w2, b2, hist_t)


def kernel(x_prompt, x_sample, cache_fox_k, cache_fox_v, cache_fox_logf, page_table, state_hgrn, state_conv,
           state_ffn_conv, norm_mix, norm_ffn, norm_final, w_in0, fox_fb, hg_lb, hg_gnorm, w_out0, w_pw1, b_pw1,
           w_dw, b_dw, ln_g, ln_b, w_pw2, b_pw2, w_ffn_in, w_ffn_dw, b_ffn_dw, w_ffn_out):
    bp, seq_len, d = x_prompt.shape
    bs, dec_seq, _ = x_sample.shape
    assert dec_seq == 1, "the sample path handles one new token per sequence"
    depth = norm_mix.shape[0]
    n_fox_heads = fox_fb.shape[1]
    fox_w = n_fox_heads * FOX_DIM
    hg_w = hg_gnorm.shape[1]
    hg_heads = hg_w // HG_DIM
    assert hg_w == fox_w, "the combined projection is split into equal-width column blocks"
    row2 = lambda a: a.reshape(1, -1)

    xp = x_prompt.reshape(bp * seq_len, d)
    xs = x_sample.reshape(bs, d)
    tm = _tile(seq_len, 512)

    fk_p, fv_p, fl_p, fk_s, fv_s, fl_s = [], [], [], [], [], []
    hs_p, hs_s, cs_p, cs_s, ffs_p, ffs_s = [], [], [], [], [], []
    for l in range(depth):
        if l % 2 == 0:
            e = l // 2
            w_main = w_in0[e][:, :7 * hg_w].astype(BF16)
            w_qkvt = w_in0[e][:, 4 * hg_w:7 * hg_w].T.astype(BF16)
            w_ff = w_in0[e][:, 7 * hg_w:].astype(BF16)
            w_out = w_out0[e].astype(BF16)
            gn = row2(hg_gnorm[e])
            fb = row2(fox_fb[e])
            q, fr, v, gate, fqt, fk, fkt, fvt, flft, ct, caug = _inproj_prompt(
                xp, row2(norm_mix[l]), w_main, w_qkvt, w_ff, fb, hg_lb, layer_e=e, batch=bp, seq_len=seq_len, tm=tm)
            o_hg, s_fin = _hgrn_prompt(q, fr, v, gate, gn, batch=bp, seq_len=seq_len, tc=tm)
            o_fox = _fox_prompt(fqt, fk, caug, fvt, ct, batch=bp, seq_len=seq_len, tk=_tile(seq_len // 4, 512))
            xp = _outproj(xp, o_hg, o_fox, w_out, tm=tm)
            to_heads = lambda a: a.reshape(bp, n_fox_heads, FOX_DIM, seq_len).transpose(0, 3, 1, 2)
            fk_p.append(to_heads(fkt))
            fv_p.append(to_heads(fvt))
            fl_p.append(flft.transpose(0, 2, 1))
            hs_p.append(s_fin)
            q, fr, v, gate, fq, fk, fv, flf = _inproj_sample(
                xs, row2(norm_mix[l]), w_main, w_ff, fb, hg_lb, layer_e=e)
            o_hg, s_fin = _hgrn_sample(q, fr, v, gate, gn, state_hgrn[e], bb=8)
            o_fox = _fox_sample(page_table, cache_fox_k.transpose(0, 1, 3, 4, 2),
                                cache_fox_v.transpose(0, 1, 3, 4, 2), cache_fox_logf.transpose(0, 1, 3, 2),
                                fq, fk, fv, flf, layer_e=e)
            xs = _outproj(xs, o_hg, o_fox, w_out, tm=bs)
            fk_s.append(fk.reshape(bs, 1, n_fox_heads, FOX_DIM))
            fv_s.append(fv.reshape(bs, 1, n_fox_heads, FOX_DIM))
            fl_s.append(flf.reshape(bs, 1, n_fox_heads))
            hs_s.append(s_fin)
        else:
            o = l // 2
            args = (row2(norm_mix[l]), w_pw1[o].astype(BF16), row2(b_pw1[o]), w_dw[o], row2(b_dw[o]),
                    row2(ln_g[o]), row2(ln_b[o]), w_pw2[o].astype(BF16), row2(b_pw2[o]))
            xp, hist = _conf_prompt(xp, *args, batch=bp, seq_len=seq_len, tm=_tile(seq_len, 256))
            cs_p.append(hist)
            xs, hist_t = _conf_sample(xs, *args, state_conv[o].transpose(1, 0, 2), bb=32)
            cs_s.append(hist_t.transpose(1, 0, 2))
        last = l == depth - 1
        fargs = (row2(norm_ffn[l]), w_ffn_in[l].astype(BF16), w_ffn_dw[l], row2(b_ffn_dw[l]),
                 w_ffn_out[l].astype(BF16), row2(norm_final))
        xp, hist = _ffn_prompt(xp, *fargs, batch=bp, seq_len=seq_len, tm=tm, final_norm=last)
        ffs_p.append(hist)
        xs, hist = _ffn_sample(xs, *fargs, state_ffn_conv[l], final_norm=last)
        ffs_s.append(hist)

    return (xp.reshape(bp, seq_len, d), xs.reshape(bs, 1, d),
            jnp.stack(fk_p), jnp.stack(fv_p), jnp.stack(fl_p),
            jnp.stack(fk_s), jnp.stack(fv_s), jnp.stack(fl_s),
            jnp.stack(hs_p), jnp.stack(hs_s), jnp.stack(cs_p), jnp.stack(cs_s),
            jnp.stack(ffs_p), jnp.stack(ffs_s))
```

```python
import functools
import math

import numpy as np
import jax
import jax.numpy as jnp
from jax import lax
from jax.experimental import pallas as pl
from jax.experimental.pallas import tpu as pltpu

F32 = jnp.float32
BF16 = jnp.bfloat16
EPS = 1e-6
HG_DIM = 128
HG_CHUNK = 64
FOX_DIM = 64
LANES = 128
SUBLANES = 8
VMEM_LIMIT = 56 * 1024 * 1024
HIGHEST = lax.Precision.HIGHEST
LOG2E = math.log2(math.e)


def _cparams(sem):
    return pltpu.CompilerParams(dimension_semantics=sem, vmem_limit_bytes=VMEM_LIMIT)


def _tile(n, target):
    t = min(n, target)
    while n % t:
        t -= 8
    return t


def _rmsnorm(x, g):
    return x * lax.rsqrt(jnp.mean(x * x, axis=-1, keepdims=True) + EPS) * g


def _silu(x):
    return x * jax.nn.sigmoid(x)


def _log_sigmoid(x):
    return -(jnp.maximum(-x, 0.0) + jnp.log1p(jnp.exp(-jnp.abs(x))))


def _dot(a, b):
    return jnp.dot(a, b, preferred_element_type=F32)


def _dot_nt(a, b):
    return lax.dot_general(a, b, (((1,), (1,)), ((), ())), preferred_element_type=F32)


def _full(shape):
    return pl.BlockSpec(shape, lambda *_: (0,) * len(shape))


def _inproj_common(h, w_ref, lb_ref, q_ref, fr_ref, v_ref, gate_ref, *, layer_e, hgw):
    def proj(k):
        return _dot(h, w_ref[:, k * hgw:(k + 1) * hgw])

    lbp = lb_ref[...]
    ex = jnp.exp(lbp - jnp.max(lbp, axis=0, keepdims=True))
    sm = ex / jnp.sum(ex, axis=0, keepdims=True)
    lb = jnp.sum(sm[:layer_e + 1], axis=0, keepdims=True)

    q_ref[...] = _silu(proj(0))
    fr_ref[...] = lb + (1.0 - lb) * jax.nn.sigmoid(proj(1))
    v_ref[...] = proj(2)
    gate_ref[...] = _silu(proj(3))
    return proj


def _split3(x):
    hi = x.astype(BF16)
    r = x - hi.astype(F32)
    mid = r.astype(BF16)
    return hi, mid, (r - mid.astype(F32)).astype(BF16)


def _inproj_prompt_body(x_ref, g_ref, w_ref, wqkvt_ref, wff_ref, wfft_ref, fb_ref, fbt_ref, lb_ref, tri_ref,
                        trit_ref, place_ref, q_ref, fr_ref, v_ref, gate_ref, fqt_ref, fk_ref, fkt_ref, fvt_ref,
                        flft_ref, ct_ref, caug_ref, carry_ref, carryt_ref, *, layer_e, hgw):
    h = _rmsnorm(x_ref[...], g_ref[...]).astype(BF16)

    @pl.when(pl.program_id(1) == 0)
    def _():
        carry_ref[...] = jnp.zeros_like(carry_ref)
        carryt_ref[...] = jnp.zeros_like(carryt_ref)

    flf = _log_sigmoid(_dot(h, wff_ref[...]) + fb_ref[...])
    c = sum(_dot(tri_ref[...], p) for p in _split3(flf)) + carry_ref[...]
    carry_ref[...] = c[-1:, :]
    parts = _split3(c * LOG2E)
    caug_ref[...] = sum(_dot(p, place_ref[i]) for i, p in enumerate(parts)).astype(BF16)
    flft = _log_sigmoid(_dot_nt(wfft_ref[...], h) + fbt_ref[...])
    flft_ref[0] = flft
    ct = sum(_dot(p, trit_ref[...]) for p in _split3(flft)) + carryt_ref[...]
    ct_ref[0] = ct
    carryt_ref[...] = ct[:, -1:]

    proj = _inproj_common(h, w_ref, lb_ref, q_ref, fr_ref, v_ref, gate_ref, layer_e=layer_e, hgw=hgw)
    fk_ref[...] = proj(5)
    qkvt = _dot_nt(wqkvt_ref[...], h)
    fqt_ref[0] = qkvt[0:hgw]
    fkt_ref[0] = qkvt[hgw:2 * hgw]
    fvt_ref[0] = qkvt[2 * hgw:]


def _caug_lane(head, part):
    return LANES * (head // 2) + 3 * (head % 2) + part


def _inproj_prompt(x, g, w_main, w_qkvt, w_ff, fb, hg_lb, *, layer_e, batch, seq_len, tm):
    m, d = x.shape
    hgw = w_qkvt.shape[0] // 3
    nfh = w_ff.shape[1]
    nt = seq_len // tm
    tri = np.tril(np.ones((tm, tm), np.float32))
    place = np.zeros((3, nfh, hgw), np.float32)
    for part in range(3):
        for head in range(nfh):
            place[part, head, _caug_lane(head, part)] = 1.0
    row = lambda n: pl.BlockSpec((tm, n), lambda b, t: (b * nt + t, 0))
    tr = lambda n: pl.BlockSpec((1, n, tm), lambda b, t: (b, 0, t))
    wide = jax.ShapeDtypeStruct((m, hgw), F32)
    wide_t = jax.ShapeDtypeStruct((batch, hgw, seq_len), F32)
    narrow_t = jax.ShapeDtypeStruct((batch, nfh, seq_len), F32)
    return pl.pallas_call(
        functools.partial(_inproj_prompt_body, layer_e=layer_e, hgw=hgw),
        grid=(batch, nt),
        in_specs=[row(d), _full((1, d)), _full(w_main.shape), _full(w_qkvt.shape), _full(w_ff.shape),
                  _full((nfh, d)), _full((1, nfh)), _full((nfh, 1)), _full(hg_lb.shape), _full((tm, tm)),
                  _full((tm, tm)), _full(place.shape)],
        out_specs=[row(hgw)] * 4 + [tr(hgw), row(hgw), tr(hgw), tr(hgw), tr(nfh), tr(nfh), row(hgw)],
        out_shape=[wide] * 4 + [wide_t, wide, wide_t, wide_t, narrow_t, narrow_t,
                                jax.ShapeDtypeStruct((m, hgw), BF16)],
        scratch_shapes=[pltpu.VMEM((1, nfh), F32), pltpu.VMEM((nfh, 1), F32)],
        compiler_params=_cparams(("arbitrary", "arbitrary")),
        name="inproj_prompt",
    )(x, g, w_main, w_qkvt, w_ff, w_ff.T, fb, fb.reshape(nfh, 1), hg_lb, jnp.asarray(tri, BF16), jnp.asarray(tri.T, BF16),
      jnp.asarray(place, BF16))


def _inproj_sample_body(x_ref, g_ref, w_ref, wff_ref, fb_ref, lb_ref,
                        q_ref, fr_ref, v_ref, gate_ref, fq_ref, fk_ref, fv_ref, flf_ref, *, layer_e, hgw):
    h = _rmsnorm(x_ref[...], g_ref[...]).astype(BF16)
    proj = _inproj_common(h, w_ref, lb_ref, q_ref, fr_ref, v_ref, gate_ref, layer_e=layer_e, hgw=hgw)
    fq_ref[...] = proj(4)
    fk_ref[...] = proj(5)
    fv_ref[...] = proj(6)
    flf_ref[...] = _log_sigmoid(_dot(h, wff_ref[...]) + fb_ref[...])


def _inproj_sample(x, g, w_main, w_ff, fb, hg_lb, *, layer_e):
    m, d = x.shape
    hgw = w_main.shape[1] // 7
    nfh = w_ff.shape[1]
    wide = jax.ShapeDtypeStruct((m, hgw), F32)
    return pl.pallas_call(
        functools.partial(_inproj_sample_body, layer_e=layer_e, hgw=hgw),
        grid=(1,),
        in_specs=[_full((m, d)), _full((1, d)), _full(w_main.shape), _full(w_ff.shape), _full((1, nfh)),
                  _full(hg_lb.shape)],
        out_specs=[_full((m, hgw))] * 7 + [_full((m, nfh))],
        out_shape=[wide] * 7 + [jax.ShapeDtypeStruct((m, nfh), F32)],
        compiler_params=_cparams(("arbitrary",)),
        name="inproj_sample",
    )(x, g, w_main, w_ff, fb, hg_lb)


def _hgrn_prompt_body(q_ref, fr_ref, v_ref, gate_ref, gn_ref, tri_ref, o_ref, s_ref,
                      st_ref, oi_ref, qb_ref, kvt_ref, dec_ref, *, n_chunks, n_heads, group):
    t = pl.program_id(1)

    @pl.when(t == 0)
    def _():
        st_ref[...] = jnp.zeros_like(st_ref)

    tri = tri_ref[...]
    rows = lax.broadcasted_iota(jnp.int32, (HG_CHUNK, HG_CHUNK), 0)
    cols = lax.broadcasted_iota(jnp.int32, (HG_CHUNK, HG_CHUNK), 1)
    causal = rows >= cols
    gn = gn_ref[...]

    heads = [slice(h * HG_DIM, (h + 1) * HG_DIM) for h in range(n_heads)]

    def local(g, carry):
        cs = [g * group + i for i in range(group)]
        rows_of = [pl.ds(pl.multiple_of(c * HG_CHUNK, HG_CHUNK), HG_CHUNK) for c in cs]
        frs = [fr_ref[rs, :] for rs in rows_of]
        splits = [_split3(jnp.log(fr)) for fr in frs]
        bs = [sum(_dot(tri, p) for p in parts) for parts in splits]
        qrs, krs, kdecs, vs = [], [], [], []
        for c, rs, fr, b in zip(cs, rows_of, frs, bs):
            q = q_ref[rs, :]
            k = 1.0 - fr
            b_last = b[HG_CHUNK - 1:HG_CHUNK, :]
            b_mid = b[HG_CHUNK // 2:HG_CHUNK // 2 + 1, :]
            kdecs.append((k * jnp.exp(b_last - b)).astype(BF16))
            qrs.append((q * jnp.exp(b - b_mid)).astype(BF16))
            krs.append((k * jnp.exp(b_mid - b)).astype(BF16))
            vs.append(v_ref[rs, :])
            qb_ref[c] = (q * jnp.exp(b)).astype(BF16)
            dec_ref[c] = jnp.exp(b_last)
        atts = [[_dot_nt(qr[:, sl], kr[:, sl]) for sl in heads] for qr, kr in zip(qrs, krs)]
        for c, v, kdec in zip(cs, vs, kdecs):
            for h, sl in enumerate(heads):
                kvt_ref[c, h] = _dot(v[:, sl].T.astype(BF16), kdec[:, sl])
        for c, v, att in zip(cs, vs, atts):
            for h, sl in enumerate(heads):
                oi_ref[c, :, sl] = _dot(jnp.where(causal, att[h], 0.0).astype(BF16), v[:, sl].astype(BF16))
        return carry

    def carried(c, carry):
        rs = pl.ds(pl.multiple_of(c * HG_CHUNK, HG_CHUNK), HG_CHUNK)
        gate = gate_ref[rs, :]
        qb = qb_ref[c]
        dec = dec_ref[c]
        for h in range(n_heads):
            sl = slice(h * HG_DIM, (h + 1) * HG_DIM)
            st = st_ref[h]
            o = oi_ref[c, :, sl] + _dot_nt(qb[:, sl], st.astype(BF16))
            st_ref[h] = st * dec[:, sl] + kvt_ref[c, h]
            o = o * lax.rsqrt(jnp.mean(o * o, axis=-1, keepdims=True) + EPS)
            o_ref[rs, sl] = (o * gn[:, sl] * gate[:, sl]).astype(o_ref.dtype)
        return carry

    lax.fori_loop(0, n_chunks // group, local, 0)
    lax.fori_loop(0, n_chunks, carried, 0, unroll=2)

    @pl.when(t == pl.num_programs(1) - 1)
    def _():
        for h in range(n_heads):
            s_ref[0, h] = st_ref[h].T


def _hgrn_prompt(q, fr, v, gate, gn, *, batch, seq_len, tc):
    m, hgw = q.shape
    n_heads = hgw // HG_DIM
    nt = seq_len // tc
    tri = jnp.asarray(np.tril(np.ones((HG_CHUNK, HG_CHUNK), np.float32)), BF16)
    n_chunks = tc // HG_CHUNK
    row = pl.BlockSpec((tc, hgw), lambda b, t: (b * nt + t, 0))
    return pl.pallas_call(
        functools.partial(_hgrn_prompt_body, n_chunks=n_chunks, n_heads=n_heads,
                          group=math.gcd(n_chunks, 4)),
        grid=(batch, nt),
        in_specs=[row, row, row, row, _full((1, hgw)), _full((HG_CHUNK, HG_CHUNK))],
        out_specs=[row, pl.BlockSpec((1, n_heads, HG_DIM, HG_DIM), lambda b, t: (b, 0, 0, 0))],
        out_shape=[jax.ShapeDtypeStruct((m, hgw), BF16),
                   jax.ShapeDtypeStruct((batch, n_heads, HG_DIM, HG_DIM), F32)],
        scratch_shapes=[pltpu.VMEM((n_heads, HG_DIM, HG_DIM), F32),
                        pltpu.VMEM((n_chunks, HG_CHUNK, hgw), F32),
                        pltpu.VMEM((n_chunks, HG_CHUNK, hgw), BF16),
                        pltpu.VMEM((n_chunks, n_heads, HG_DIM, HG_DIM), F32),
                        pltpu.VMEM((n_chunks, 1, hgw), F32)],
        compiler_params=_cparams(("arbitrary", "arbitrary")),
        name="hgrn_prompt",
    )(q, fr, v, gate, gn, tri)


def _hgrn_sample_body(q_ref, fr_ref, v_ref, gate_ref, gn_ref, s_ref, o_ref, so_ref, *, bb, n_heads):
    def col(r):
        return jnp.broadcast_to(r, (HG_DIM, HG_DIM)).T

    for i in range(bb):
        for h in range(n_heads):
            sl = slice(h * HG_DIM, (h + 1) * HG_DIM)
            fr = fr_ref[i:i + 1, sl]
            dec = jnp.exp(jnp.log(fr))
            s_new = col(dec) * s_ref[i, h] + col(1.0 - fr) * v_ref[i:i + 1, sl]
            so_ref[i, h] = s_new
            o = jnp.sum(col(q_ref[i:i + 1, sl]) * s_new, axis=0, keepdims=True)
            o = o * lax.rsqrt(jnp.mean(o * o, axis=-1, keepdims=True) + EPS)
            o_ref[i:i + 1, sl] = (o * gn_ref[:, sl] * gate_ref[i:i + 1, sl]).astype(o_ref.dtype)


def _hgrn_sample(q, fr, v, gate, gn, state, *, bb):
    m, hgw = q.shape
    n_heads = hgw // HG_DIM
    row = pl.BlockSpec((bb, hgw), lambda i: (i, 0))
    st = pl.BlockSpec((bb, n_heads, HG_DIM, HG_DIM), lambda i: (i, 0, 0, 0))
    return pl.pallas_call(
        functools.partial(_hgrn_sample_body, bb=bb, n_heads=n_heads),
        grid=(m // bb,),
        in_specs=[row, row, row, row, _full((1, hgw)), st],
        out_specs=[row, st],
        out_shape=[jax.ShapeDtypeStruct((m, hgw), BF16), jax.ShapeDtypeStruct(state.shape, F32)],
        compiler_params=_cparams(("arbitrary",)),
        name="hgrn_sample",
    )(q, fr, v, gate, gn, state)


def _fox_prompt_body(qi_ref, ki_ref, qt_ref, k_ref, ca_ref, vt_ref, ct_ref, o_ref, m_ref, acc_ref, *, tk):
    j = pl.program_id(2)
    qi = qi_ref[j]
    ki = ki_ref[j]
    tq = 2 * tk

    @pl.when(ki == 0)
    def _():
        m_ref[...] = jnp.full_like(m_ref, -jnp.inf)
        acc_ref[...] = jnp.zeros_like(acc_ref)

    sub = lax.broadcasted_iota(jnp.int32, (LANES, 1), 0)

    def update(units, masked):
        qt = qt_ref[0] * (FOX_DIM ** -0.5 * LOG2E)
        kaug = jnp.concatenate([k_ref[0].astype(BF16), ca_ref[0]], axis=1)
        vt = vt_ref[0]
        vaug = {hh: jnp.where(sub // FOX_DIM == hh, vt, 1.0).astype(BF16) for hh in sorted({u[0] for u in units})}
        keep = (lax.broadcasted_iota(jnp.int32, (tk, tk), 1)
                >= lax.broadcasted_iota(jnp.int32, (tk, tk), 0))
        lanes = lambda half: slice(half * tk, (half + 1) * tk)
        m_prev = {u: m_ref[u[0], :, lanes(u[1])] for u in units}
        acc_prev = {u: acc_ref[u[0], :, lanes(u[1])] for u in units}
        qaug = {}
        for hh in sorted({u[0] for u in units}):
            qm = jnp.where(sub // FOX_DIM == hh, qt, 0.0)
            neg = jnp.where((sub >= 3 * hh) & (sub < 3 * hh + 3), -1.0, 0.0)
            qaug[hh] = jnp.concatenate([qm, jnp.broadcast_to(neg, (LANES, tq))], axis=0).astype(BF16)
        s_all = {u: _dot(kaug, qaug[u[0]][:, lanes(u[1])]) for u in units}
        m_out, acc_out = {}, {}
        for u in units:
            hh, half = u
            s = jnp.where(keep, s_all[u], -jnp.inf) if u in masked else s_all[u]
            ct = ct_ref[0, 0, hh:hh + 1, lanes(half)] * LOG2E
            m_new = jnp.maximum(m_prev[u], jnp.max(s, axis=0, keepdims=True) + ct)
            p = jnp.exp2(s - (m_new - ct))
            alpha = jnp.exp2(m_prev[u] - m_new)
            acc_out[u] = alpha * acc_prev[u] + _dot(vaug[hh], p.astype(BF16))
            m_out[u] = m_new
        for u in units:
            acc_ref[u[0], :, lanes(u[1])] = acc_out[u]
            m_ref[u[0], :, lanes(u[1])] = m_out[u]

    every = ((0, 0), (1, 0), (0, 1), (1, 1))
    upper = ((0, 1), (1, 1))

    @pl.when(ki < 2 * qi)
    def _():
        update(every, ())

    @pl.when(ki == 2 * qi)
    def _():
        update(every, ((0, 0), (1, 0)))

    @pl.when(ki == 2 * qi + 1)
    def _():
        update(upper, upper)
        a0 = acc_ref[0]
        a1 = acc_ref[1]
        o0 = a0 * (1.0 / a0[FOX_DIM:FOX_DIM + 1])
        o1 = a1 * (1.0 / a1[0:1])
        o_ref[0] = jnp.where(sub // FOX_DIM == 0, o0, o1).T.astype(o_ref.dtype)


def _fox_prompt(fqt, fk, caug, fvt, ct, *, batch, seq_len, tk):
    m, fw = fk.shape
    n_pairs = fw // LANES
    tq = 2 * tk
    nq = seq_len // tq
    qi_tab = np.array([q for q in range(nq) for _ in range(2 * q + 2)], np.int32)
    ki_tab = np.array([k for q in range(nq) for k in range(2 * q + 2)], np.int32)
    r3 = lambda a: a.reshape(batch, seq_len, fw)
    by_q = lambda b, p, j, qt, kt: (b, p, qt[j])
    by_k = lambda b, p, j, qt, kt: (b, kt[j], p)
    grid_spec = pltpu.PrefetchScalarGridSpec(
        num_scalar_prefetch=2,
        grid=(batch, n_pairs, len(qi_tab)),
        in_specs=[
            pl.BlockSpec((1, LANES, tq), by_q),
            pl.BlockSpec((1, tk, LANES), by_k),
            pl.BlockSpec((1, tk, LANES), by_k),
            pl.BlockSpec((1, LANES, tk), lambda b, p, j, qt, kt: (b, p, kt[j])),
            pl.BlockSpec((1, 1, 2, tq), lambda b, p, j, qt, kt: (b, p, 0, qt[j])),
        ],
        out_specs=pl.BlockSpec((1, tq, LANES), lambda b, p, j, qt, kt: (b, qt[j], p)),
        scratch_shapes=[pltpu.VMEM((2, 1, tq), F32), pltpu.VMEM((2, LANES, tq), F32)],
    )
    o = pl.pallas_call(
        functools.partial(_fox_prompt_body, tk=tk),
        grid_spec=grid_spec,
        out_shape=jax.ShapeDtypeStruct((batch, seq_len, fw), BF16),
        compiler_params=_cparams(("arbitrary", "arbitrary", "arbitrary")),
        name="fox_prompt",
    )(jnp.asarray(qi_tab), jnp.asarray(ki_tab), fqt, r3(fk), r3(caug), fvt,
      ct.reshape(batch, n_pairs, 2, seq_len))
    return o.reshape(m, fw)


def _fox_sample_stages(k_refs, v_refs, lf_refs, q_ref, kn_ref, vn_ref, cn_ref, o_ref,
                       qcol_ref, bias_ref, s_ref, p_ref, *, n_heads):
    n_pages = len(k_refs)
    page = lf_refs[0].shape[-1]
    width = n_heads * FOX_DIM
    rows = n_pages * n_heads
    hpb = LANES // FOX_DIM
    n_blocks = n_heads // hpb
    st = {}

    def block_of(ref, b):
        return ref[b * LANES:(b + 1) * LANES, :]

    def prepare():
        q_row = q_ref[0] * FOX_DIM ** -0.5
        st["diag"] = (lax.broadcasted_iota(jnp.int32, (n_heads, width), 1) // FOX_DIM
                      == lax.broadcasted_iota(jnp.int32, (n_heads, width), 0))
        st["s_new"] = jnp.sum(jnp.where(st["diag"], q_row, 0.0) * kn_ref[0], axis=-1, keepdims=True)
        for b in range(n_blocks):
            qcol_ref[b] = jnp.broadcast_to(q_row[:, b * LANES:(b + 1) * LANES], (LANES, LANES)).T
        lf = jnp.concatenate([r[...] for r in lf_refs], axis=0)
        pj = lax.broadcasted_iota(jnp.int32, (page, page), 0)
        ps = lax.broadcasted_iota(jnp.int32, (page, page), 1)
        after = jnp.where(pj > ps, 1.0, 0.0).astype(BF16)
        rr = lax.broadcasted_iota(jnp.int32, (rows, rows), 0)
        rc = lax.broadcasted_iota(jnp.int32, (rows, rows), 1)
        later_pages = jnp.where(rc % n_heads == rr % n_heads,
                                jnp.where(rc // n_heads > rr // n_heads, 1.0, 0.0), 0.0).astype(BF16)
        tot = jnp.broadcast_to(jnp.sum(lf, axis=-1, keepdims=True), (rows, page))
        bias = (sum(_dot(p, after) for p in _split3(lf)) + sum(_dot(later_pages, p) for p in _split3(tot))
                + jnp.concatenate([cn_ref[0]] * n_pages, axis=0))
        bias_ref[...] = bias.reshape(n_pages, n_heads, page)

    def scores(b):
        qcol = qcol_ref[b]
        for g in range(n_pages):
            prod = block_of(k_refs[g], b) * qcol
            s_ref[g, b * hpb:(b + 1) * hpb, :] = jnp.sum(prod.reshape(hpb, FOX_DIM, page), axis=1)

    def softmax():
        s = s_ref[...] + bias_ref[...]
        m = jnp.maximum(jnp.max(jnp.max(s, axis=0), axis=-1, keepdims=True), st["s_new"])
        p_new = jnp.exp(st["s_new"] - m)
        p = jnp.exp(s - m)
        inv_l = 1.0 / (p_new + jnp.sum(jnp.sum(p, axis=0), axis=-1, keepdims=True))
        p_ref[...] = p * inv_l
        own_new = jnp.sum(jnp.where(st["diag"], p_new * inv_l, 0.0), axis=0, keepdims=True)
        o_ref[0] = own_new * vn_ref[0]

    def values(b):
        acc = jnp.zeros((LANES, page), F32)
        for g in range(n_pages):
            pb = jnp.broadcast_to(p_ref[g, b * hpb:(b + 1) * hpb, :][:, None, :], (hpb, FOX_DIM, page))
            acc = acc + block_of(v_refs[g], b) * pb.reshape(LANES, page)
        lanes = slice(b * LANES, (b + 1) * LANES)
        o_ref[0, :, lanes] = o_ref[0, :, lanes] + jnp.sum(acc.T, axis=0, keepdims=True)

    return ([prepare] + [functools.partial(scores, b) for b in range(n_blocks)] + [softmax]
            + [functools.partial(values, b) for b in range(n_blocks)])


def _fox_sample_scratch(n_pages, n_heads, page):
    return [pltpu.VMEM((n_heads * FOX_DIM // LANES, LANES, LANES), F32), pltpu.VMEM((n_pages, n_heads, page), F32),
            pltpu.VMEM((n_pages, n_heads, page), F32), pltpu.VMEM((n_pages, n_heads, page), F32)]


def _outproj_body(x_ref, a_ref, b_ref, w_ref, o_ref):
    ka = a_ref.shape[1]
    o_ref[...] = (x_ref[...] + _dot(a_ref[...].astype(BF16), w_ref[:ka, :])
                  + _dot(b_ref[...].astype(BF16), w_ref[ka:, :]))


def _outproj(x, a, b, w, *, tm):
    m, d = x.shape
    row = lambda n: pl.BlockSpec((tm, n), lambda i: (i, 0))
    return pl.pallas_call(
        _outproj_body, grid=(m // tm,),
        in_specs=[row(d), row(a.shape[1]), row(b.shape[1]), _full(w.shape)],
        out_specs=row(d), out_shape=jax.ShapeDtypeStruct((m, d), F32),
        compiler_params=_cparams(("arbitrary",)), name="outproj",
    )(x, a, b, w)


def _gelu(x):
    return 0.5 * x * (1.0 + lax.erf(x * (2.0 ** -0.5)))


def _ffn_cols(ffn_dim):
    cw = 256
    while ffn_dim % cw:
        cw //= 2
    return cw


def _ffn_prompt_body(pt_ref, x_ref, g_ref, win_ref, wdw_ref, bdw_ref, wout_ref, gfin_ref, *refs,
                     tm, ffn_dim, cw, final_norm, n_pages, n_heads, mixed):
    mix_refs, refs = (refs[:3], refs[3:]) if mixed else ((), refs)
    (k_hbm, v_hbm, lf_hbm, q_ref, kn_ref, vn_ref, cn_ref, o_ref, hist_ref, os_ref, ext_ref, carry_ref, act_ref,
     qcol_ref, bias_ref, s_ref, p_ref, k_buf, v_buf, lf_buf, sem) = refs
    t = pl.program_id(1)
    step = pl.program_id(0) * pl.num_programs(1) + t
    last_step = pl.num_programs(0) * pl.num_programs(1) - 1
    slot = step % 2

    @pl.when(t == 0)
    def _():
        carry_ref[...] = jnp.zeros_like(carry_ref)

    def page_copies(row, to_slot, kind):
        hbm, buf = ((k_hbm, k_buf), (v_hbm, v_buf), (lf_hbm, lf_buf))[kind]
        return [pltpu.make_async_copy(hbm.at[pt_ref[row * n_pages + g]], buf.at[to_slot, g], sem.at[to_slot, kind])
                for g in range(n_pages)]

    def start_row(row, to_slot):
        for kind in range(3):
            for c in page_copies(row, to_slot, kind):
                c.start()

    def wait_row(row, from_slot, kind):
        for c in page_copies(row, from_slot, kind):
            c.wait()

    @pl.when(step == 0)
    def _():
        start_row(0, 0)

    next_row = jnp.minimum(step + 1, last_step)
    start_row(next_row, 1 - slot)

    attend = _fox_sample_stages([k_buf.at[slot, g] for g in range(n_pages)], [v_buf.at[slot, g] for g in range(n_pages)],
                                [lf_buf.at[slot, g] for g in range(n_pages)], q_ref, kn_ref, vn_ref, cn_ref, os_ref,
                                qcol_ref, bias_ref, s_ref, p_ref, n_heads=n_heads)
    first_of = {2: 0, 0: 1, 1: 2 + n_heads * FOX_DIM // LANES}
    stages = []
    for idx, stage in enumerate(attend):
        for kind, first in first_of.items():
            if idx == first:
                stages.append(functools.partial(wait_row, step, slot, kind))
        stages.append(stage)
    n_cols = ffn_dim // cw
    per_col = -(-len(stages) // n_cols)

    x = x_ref[...]
    if mixed:
        a_ref, b_ref, wmix_ref = mix_refs
        ka = a_ref.shape[1]
        x = x + _dot(a_ref[...], wmix_ref[:ka, :]) + _dot(b_ref[...], wmix_ref[ka:, :])
    h = _rmsnorm(x, g_ref[...]).astype(BF16)
    for j in range(ffn_dim // cw):
        cs = slice(j * cw, (j + 1) * cw)
        g = _dot(h, win_ref[:, cs])
        u = _dot(h, win_ref[:, ffn_dim + j * cw:ffn_dim + (j + 1) * cw])
        ext_ref[6:8, :] = carry_ref[6:8, cs]
        ext_ref[8:, :] = g
        carry_ref[6:8, cs] = g[tm - 2:, :]
        gc = (wdw_ref[0:1, cs] * ext_ref[6:6 + tm, :] + wdw_ref[1:2, cs] * ext_ref[7:7 + tm, :]
              + wdw_ref[2:3, cs] * g + bdw_ref[:, cs])
        act_ref[:, cs] = (_gelu(gc) * u).astype(BF16)
        for stage in stages[j * per_col:(j + 1) * per_col]:
            stage()
    y = x + _dot(act_ref[...], wout_ref[...])
    if final_norm:
        y = _rmsnorm(y, gfin_ref[...])
    o_ref[...] = y

    @pl.when(t == pl.num_programs(1) - 1)
    def _():
        hist_ref[0] = carry_ref[6:8, :]

    @pl.when(step == last_step)
    def _():
        for kind in range(3):
            wait_row(next_row, 1 - slot, kind)


def _ffn_prompt(x, g, w_in, w_dw, b_dw, w_out, g_final, sample, mix, *, batch, seq_len, tm, final_norm):
    m, d = x.shape
    ffn_dim = w_out.shape[0]
    cw = _ffn_cols(ffn_dim)
    nt = seq_len // tm
    page_table, cache_kt, cache_vt, cache_lft, fq, fk, fv, flf, layer_e, first_row = sample
    bs, n_pages = page_table.shape
    n_layers, n_pool, n_heads, _, page = cache_kt.shape
    assert page == LANES
    width = n_heads * FOX_DIM
    srow = lambda b, t: first_row + b * nt + t
    pages_of = lambda c: c.reshape((n_layers * n_pool, -1, page))
    pt_rows = page_table[first_row:first_row + batch * nt] + layer_e * n_pool

    in_hbm = pl.BlockSpec(memory_space=pl.ANY)

    def const(shape):
        return pl.BlockSpec(shape, lambda b, t, pt: (0,) * len(shape), pipeline_mode=pl.Buffered(1))

    rows_of = lambda n: pl.BlockSpec((tm, n), lambda b, t, pt: (b * nt + t, 0))
    row = rows_of(d)
    tok = pl.BlockSpec((1, 1, width), lambda b, t, pt: (srow(b, t), 0, 0))
    r3 = lambda a: a.reshape(bs, 1, width)
    mix = () if mix is None else tuple(mix)
    mix_specs = [rows_of(mix[0].shape[1]), rows_of(mix[1].shape[1]), const(mix[2].shape)] if mix else []
    grid_spec = pltpu.PrefetchScalarGridSpec(
        num_scalar_prefetch=1, grid=(batch, nt),
        in_specs=[row, const((1, d)), const(w_in.shape), const(w_dw.shape), const((1, ffn_dim)),
                  const(w_out.shape), const((1, d))] + mix_specs
        + [in_hbm, in_hbm, in_hbm, tok, tok, tok,
           pl.BlockSpec((1, n_heads, 1), lambda b, t, pt: (srow(b, t), 0, 0))],
        out_specs=[row, pl.BlockSpec((1, 2, ffn_dim), lambda b, t, pt: (b, 0, 0)),
                   pl.BlockSpec((1, 1, width), lambda b, t, pt: (b * nt + t, 0, 0))],
        scratch_shapes=[pltpu.VMEM((tm + 8, cw), F32), pltpu.VMEM((8, ffn_dim), F32),
                        pltpu.VMEM((tm, ffn_dim), BF16)] + _fox_sample_scratch(n_pages, n_heads, page)
        + [pltpu.VMEM((2, n_pages, width, page), F32), pltpu.VMEM((2, n_pages, width, page), F32),
           pltpu.VMEM((2, n_pages, n_heads, page), F32), pltpu.SemaphoreType.DMA((2, 3))])
    y, hist, o_s = pl.pallas_call(
        functools.partial(_ffn_prompt_body, tm=tm, ffn_dim=ffn_dim, cw=cw, final_norm=final_norm,
                          n_pages=n_pages, n_heads=n_heads, mixed=bool(mix)),
        grid_spec=grid_spec,
        out_shape=[jax.ShapeDtypeStruct((m, d), F32), jax.ShapeDtypeStruct((batch, 2, ffn_dim), F32),
                   jax.ShapeDtypeStruct((batch * nt, 1, width), F32)],
        compiler_params=_cparams(("arbitrary", "arbitrary")),
        name="ffn_prompt",
    )(pt_rows.reshape(-1), x, g, w_in, w_dw, b_dw, w_out, g_final, *mix,
      pages_of(cache_kt), pages_of(cache_vt), pages_of(cache_lft),
      r3(fq), r3(fk), r3(fv), flf.reshape(bs, n_heads, 1))
    return y, hist, o_s.reshape(batch * nt, width)


def _ffn_sample_body(x_ref, g_ref, win_ref, wdw_ref, bdw_ref, wout_ref, gfin_ref, hist_ref,
                     o_ref, nh_ref, act_ref, *, ffn_dim, cw, final_norm):
    x = x_ref[...]
    h = _rmsnorm(x, g_ref[...]).astype(BF16)
    for j in range(ffn_dim // cw):
        cs = slice(j * cw, (j + 1) * cw)
        g = _dot(h, win_ref[:, cs])
        u = _dot(h, win_ref[:, ffn_dim + j * cw:ffn_dim + (j + 1) * cw])
        h0 = hist_ref[:, cs]
        h1 = hist_ref[:, ffn_dim + j * cw:ffn_dim + (j + 1) * cw]
        gc = wdw_ref[0:1, cs] * h0 + wdw_ref[1:2, cs] * h1 + wdw_ref[2:3, cs] * g + bdw_ref[:, cs]
        act_ref[:, cs] = (_gelu(gc) * u).astype(BF16)
        nh_ref[:, cs] = h1
        nh_ref[:, ffn_dim + j * cw:ffn_dim + (j + 1) * cw] = g
    y = x + _dot(act_ref[...], wout_ref[...])
    if final_norm:
        y = _rmsnorm(y, gfin_ref[...])
    o_ref[...] = y


def _ffn_sample(x, g, w_in, w_dw, b_dw, w_out, g_final, hist, *, final_norm):
    m, d = x.shape
    ffn_dim = w_out.shape[0]
    cw = _ffn_cols(ffn_dim)
    hist2 = hist.reshape(m, 2 * ffn_dim)
    y, nh = pl.pallas_call(
        functools.partial(_ffn_sample_body, ffn_dim=ffn_dim, cw=cw, final_norm=final_norm),
        grid=(1,),
        in_specs=[_full((m, d)), _full((1, d)), _full(w_in.shape), _full(w_dw.shape),
                  _full((1, ffn_dim)), _full(w_out.shape), _full((1, d)), _full(hist2.shape)],
        out_specs=[_full((m, d)), _full(hist2.shape)],
        out_shape=[jax.ShapeDtypeStruct((m, d), F32), jax.ShapeDtypeStruct(hist2.shape, F32)],
        scratch_shapes=[pltpu.VMEM((m, ffn_dim), BF16)],
        compiler_params=_cparams(("arbitrary",)),
        name="ffn_sample",
    )(x, g, w_in, w_dw, b_dw, w_out, g_final, hist2)
    return y, nh.reshape(m, 2, ffn_dim)


def _layernorm_silu(d, g, b):
    dc = d - jnp.mean(d, axis=-1, keepdims=True)
    y = dc * lax.rsqrt(jnp.mean(dc * dc, axis=-1, keepdims=True) + EPS) * g + b
    return _silu(y)


def _conf_prompt_body(x_ref, g_ref, w1_ref, b1_ref, wdw_ref, bdw_ref, lng_ref, lnb_ref, w2_ref, b2_ref,
                      o_ref, hist_ref, ext_ref, ph_ref, *, tm, ch, cw, pad):
    t = pl.program_id(1)

    @pl.when(t == 0)
    def _():
        ext_ref[0:pad, :] = jnp.zeros((pad, ch), F32)

    x = x_ref[...]
    h = _rmsnorm(x, g_ref[...]).astype(BF16)
    a = _dot(h, w1_ref[:, :ch]) + b1_ref[:, :ch]
    gt = _dot(h, w1_ref[:, ch:]) + b1_ref[:, ch:]
    ext_ref[pad:, :] = a * jax.nn.sigmoid(gt)
    span = ph_ref.shape[1]
    for s in range(1, SUBLANES):
        ph_ref[s - 1] = ext_ref[s:s + span, :]
    off = pad - (cw - 1)
    d = bdw_ref[...]
    for j in range(cw):
        base, s = (off + j) // SUBLANES * SUBLANES, (off + j) % SUBLANES
        src = ext_ref[base:base + tm, :] if s == 0 else ph_ref[s - 1, base:base + tm, :]
        d = d + wdw_ref[j:j + 1, :] * src
    y = _layernorm_silu(d, lng_ref[...], lnb_ref[...]).astype(BF16)
    o_ref[...] = x + _dot(y, w2_ref[...]) + b2_ref[...]

    @pl.when(t == pl.num_programs(1) - 1)
    def _():
        hist_ref[0] = ext_ref[tm + off:tm + pad, :]

    ext_ref[0:pad, :] = ext_ref[tm:tm + pad, :]


def _conf_prompt(x, g, w1, b1, wdw, bdw, lng, lnb, w2, b2, *, batch, seq_len, tm):
    m, d = x.shape
    cw, ch = wdw.shape
    pad = -(-(cw - 1) // SUBLANES) * SUBLANES
    nt = seq_len // tm
    row = pl.BlockSpec((tm, d), lambda b, t: (b * nt + t, 0))
    return pl.pallas_call(
        functools.partial(_conf_prompt_body, tm=tm, ch=ch, cw=cw, pad=pad),
        grid=(batch, nt),
        in_specs=[row, _full((1, d)), _full(w1.shape), _full((1, 2 * ch)), _full(wdw.shape), _full((1, ch)),
                  _full((1, ch)), _full((1, ch)), _full(w2.shape), _full((1, d))],
        out_specs=[row, pl.BlockSpec((1, cw - 1, ch), lambda b, t: (b, 0, 0))],
        out_shape=[jax.ShapeDtypeStruct((m, d), F32), jax.ShapeDtypeStruct((batch, cw - 1, ch), F32)],
        scratch_shapes=[pltpu.VMEM((tm + pad, ch), F32),
                        pltpu.VMEM((SUBLANES - 1, tm + pad - SUBLANES, ch), F32)],
        compiler_params=_cparams(("arbitrary", "arbitrary")),
        name="conf_prompt",
    )(x, g, w1, b1, wdw, bdw, lng, lnb, w2, b2)


def _conf_sample_body(x_ref, g_ref, w1_ref, b1_ref, wdw_ref, bdw_ref, lng_ref, lnb_ref, w2_ref, b2_ref,
                      hist_ref, o_ref, nh_ref, *, ch, cw):
    x = x_ref[...]
    h = _rmsnorm(x, g_ref[...]).astype(BF16)
    a = _dot(h, w1_ref[:, :ch]) + b1_ref[:, :ch]
    gt = _dot(h, w1_ref[:, ch:]) + b1_ref[:, ch:]
    u = a * jax.nn.sigmoid(gt)
    d = bdw_ref[...] + wdw_ref[cw - 1:cw, :] * u
    for j in range(cw - 1):
        d = d + wdw_ref[j:j + 1, :] * hist_ref[j]
    nh_ref[0:cw - 2] = hist_ref[1:cw - 1]
    nh_ref[cw - 2] = u
    y = _layernorm_silu(d, lng_ref[...], lnb_ref[...]).astype(BF16)
    o_ref[...] = x + _dot(y, w2_ref[...]) + b2_ref[...]


def _conf_sample(x, g, w1, b1, wdw, bdw, lng, lnb, w2, b2, hist_t, *, bb):
    m, d = x.shape
    cw, ch = wdw.shape
    row = lambda n: pl.BlockSpec((bb, n), lambda i: (i, 0))
    hs = pl.BlockSpec((cw - 1, bb, ch), lambda i: (0, i, 0))
    return pl.pallas_call(
        functools.partial(_conf_sample_body, ch=ch, cw=cw),
        grid=(m // bb,),
        in_specs=[row(d), _full((1, d)), _full(w1.shape), _full((1, 2 * ch)), _full(wdw.shape), _full((1, ch)),
                  _full((1, ch)), _full((1, ch)), _full(w2.shape), _full((1, d)), hs],
        out_specs=[row(d), hs],
        out_shape=[jax.ShapeDtypeStruct((m, d), F32), jax.ShapeDtypeStruct(hist_t.shape, F32)],
        compiler_params=_cparams(("arbitrary",)),
        name="conf_sample",
    )(x, g, w1, b1, wdw, bdw, lng, lnb, w2, b2, hist_t)


def kernel(x_prompt, x_sample, cache_fox_k, cache_fox_v, cache_fox_logf, page_table, state_hgrn, state_conv,
           state_ffn_conv, norm_mix, norm_ffn, norm_final, w_in0, fox_fb, hg_lb, hg_gnorm, w_out0, w_pw1, b_pw1,
           w_dw, b_dw, ln_g, ln_b, w_pw2, b_pw2, w_ffn_in, w_ffn_dw, b_ffn_dw, w_ffn_out):
    bp, seq_len, d = x_prompt.shape
    bs, dec_seq, _ = x_sample.shape
    assert dec_seq == 1, "the sample path handles one new token per sequence"
    depth = norm_mix.shape[0]
    n_fox_heads = fox_fb.shape[1]
    fox_w = n_fox_heads * FOX_DIM
    hg_w = hg_gnorm.shape[1]
    hg_heads = hg_w // HG_DIM
    assert hg_w == fox_w, "the combined projection is split into equal-width column blocks"
    row2 = lambda a: a.reshape(1, -1)

    xp = x_prompt.reshape(bp * seq_len, d)
    xs = x_sample.reshape(bs, d)
    tm = _tile(seq_len, 512)

    assert depth == 2, "the sample attention is scheduled across exactly two prompt ConvFFN calls"
    rows_per_ffn = bs // depth
    tm_ffn = seq_len * bp // rows_per_ffn
    assert rows_per_ffn * depth == bs and tm_ffn * rows_per_ffn == seq_len * bp and seq_len % tm_ffn == 0

    def conf_args(l):
        o = l // 2
        return (row2(norm_mix[l]), w_pw1[o].astype(BF16), row2(b_pw1[o]), w_dw[o], row2(b_dw[o]),
                row2(ln_g[o]), row2(ln_b[o]), w_pw2[o].astype(BF16), row2(b_pw2[o]))

    def ffn_args(l):
        return (row2(norm_ffn[l]), w_ffn_in[l].astype(BF16), w_ffn_dw[l], row2(b_ffn_dw[l]),
                w_ffn_out[l].astype(BF16), row2(norm_final))

    fk_p, fv_p, fl_p, fk_s, fv_s, fl_s = [], [], [], [], [], []
    hs_p, hs_s, cs_p, cs_s, ffs_p, ffs_s, o_fox_s = [], [], [], [], [], [], []
    for l in range(depth):
        if l % 2 == 0:
            e = l // 2
            w_main = w_in0[e][:, :7 * hg_w].astype(BF16)
            w_qkvt = w_in0[e][:, 4 * hg_w:7 * hg_w].T.astype(BF16)
            w_ff = w_in0[e][:, 7 * hg_w:].astype(BF16)
            w_out = w_out0[e].astype(BF16)
            gn = row2(hg_gnorm[e])
            fb = row2(fox_fb[e])
            q, fr, v, gate, fqt, fk, fkt, fvt, flft, ct, caug = _inproj_prompt(
                xp, row2(norm_mix[l]), w_main, w_qkvt, w_ff, fb, hg_lb, layer_e=e, batch=bp, seq_len=seq_len, tm=tm)
            o_hg, s_fin = _hgrn_prompt(q, fr, v, gate, gn, batch=bp, seq_len=seq_len, tc=tm)
            o_fox = _fox_prompt(fqt, fk, caug, fvt, ct, batch=bp, seq_len=seq_len, tk=_tile(seq_len // 4, 512))
            mix = (o_hg, o_fox, w_out)
            to_heads = lambda a: a.reshape(bp, n_fox_heads, FOX_DIM, seq_len).transpose(0, 3, 1, 2)
            fk_p.append(to_heads(fkt))
            fv_p.append(to_heads(fvt))
            fl_p.append(flft.transpose(0, 2, 1))
            hs_p.append(s_fin)
            q, fr, v, gate, fq, fk, fv, flf = _inproj_sample(
                xs, row2(norm_mix[l]), w_main, w_ff, fb, hg_lb, layer_e=e)
            o_hg_s, s_fin = _hgrn_sample(q, fr, v, gate, gn, state_hgrn[e], bb=8)
            paged = (page_table, cache_fox_k.transpose(0, 1, 3, 4, 2), cache_fox_v.transpose(0, 1, 3, 4, 2),
                     cache_fox_logf.transpose(0, 1, 3, 2), fq, fk, fv, flf, e)
            fk_s.append(fk.reshape(bs, 1, n_fox_heads, FOX_DIM))
            fv_s.append(fv.reshape(bs, 1, n_fox_heads, FOX_DIM))
            fl_s.append(flf.reshape(bs, 1, n_fox_heads))
            hs_s.append(s_fin)
        else:
            xp, hist = _conf_prompt(xp, *conf_args(l), batch=bp, seq_len=seq_len, tm=_tile(seq_len, 256))
            cs_p.append(hist)
        xp, hist, o_part = _ffn_prompt(xp, *ffn_args(l), paged + (l * rows_per_ffn,), mix if l % 2 == 0 else None,
                                       batch=bp, seq_len=seq_len, tm=tm_ffn, final_norm=l == depth - 1)
        ffs_p.append(hist)
        o_fox_s.append(o_part)

    for l in range(depth):
        if l % 2 == 0:
            xs = _outproj(xs, o_hg_s, jnp.concatenate(o_fox_s, axis=0), w_out, tm=bs)
        else:
            xs, hist_t = _conf_sample(xs, *conf_args(l), state_conv[l // 2].transpose(1, 0, 2), bb=min(bs, 32))
            cs_s.append(hist_t.transpose(1, 0, 2))
        xs, hist = _ffn_sample(xs, *ffn_args(l), state_ffn_conv[l], final_norm=l == depth - 1)
        ffs_s.append(hist)

    return (xp.reshape(bp, seq_len, d), xs.reshape(bs, 1, d),
            jnp.stack(fk_p), jnp.stack(fv_p), jnp.stack(fl_p),
            jnp.stack(fk_s), jnp.stack(fv_s), jnp.stack(fl_s),
            jnp.stack(hs_p), jnp.stack(hs_s), jnp.stack(cs_p), jnp.stack(cs_s),
            jnp.stack(ffs_p), jnp.stack(ffs_s))
```

```python
import functools
import math

import numpy as np
import jax
import jax.numpy as jnp
from jax import lax
from jax.experimental import pallas as pl
from jax.experimental.pallas import tpu as pltpu

F32 = jnp.float32
BF16 = jnp.bfloat16
EPS = 1e-6
HG_DIM = 128
HG_CHUNK = 64
FOX_DIM = 64
LANES = 128
SUBLANES = 8
VMEM_LIMIT = 56 * 1024 * 1024
HIGHEST = lax.Precision.HIGHEST
LOG2E = math.log2(math.e)


def _cparams(sem):
    return pltpu.CompilerParams(dimension_semantics=sem, vmem_limit_bytes=VMEM_LIMIT)


def _tile(n, target):
    t = min(n, target)
    while n % t:
        t -= 8
    return t


def _rmsnorm(x, g):
    return x * lax.rsqrt(jnp.mean(x * x, axis=-1, keepdims=True) + EPS) * g


def _silu(x):
    return x * jax.nn.sigmoid(x)


def _log_sigmoid(x):
    return -(jnp.maximum(-x, 0.0) + jnp.log1p(jnp.exp(-jnp.abs(x))))


def _dot(a, b):
    return jnp.dot(a, b, preferred_element_type=F32)


def _dot_nt(a, b):
    return lax.dot_general(a, b, (((1,), (1,)), ((), ())), preferred_element_type=F32)


def _full(shape):
    return pl.BlockSpec(shape, lambda *_: (0,) * len(shape))


def _inproj_common(h, w_ref, lb_ref, q_ref, fr_ref, v_ref, gate_ref, *, layer_e, hgw):
    def proj(k):
        return _dot(h, w_ref[:, k * hgw:(k + 1) * hgw])

    lbp = lb_ref[...]
    ex = jnp.exp(lbp - jnp.max(lbp, axis=0, keepdims=True))
    sm = ex / jnp.sum(ex, axis=0, keepdims=True)
    lb = jnp.sum(sm[:layer_e + 1], axis=0, keepdims=True)

    q_ref[...] = _silu(proj(0))
    fr_ref[...] = lb + (1.0 - lb) * jax.nn.sigmoid(proj(1))
    v_ref[...] = proj(2)
    gate_ref[...] = _silu(proj(3))
    return proj


def _split3(x):
    hi = x.astype(BF16)
    r = x - hi.astype(F32)
    mid = r.astype(BF16)
    return hi, mid, (r - mid.astype(F32)).astype(BF16)


def _inproj_prompt_body(x_ref, g_ref, w_ref, wqkvt_ref, wff_ref, wfft_ref, fb_ref, fbt_ref, lb_ref, tri_ref,
                        trit_ref, place_ref, q_ref, fr_ref, v_ref, gate_ref, fqt_ref, fk_ref, fkt_ref, fvt_ref,
                        flft_ref, ct_ref, caug_ref, carry_ref, carryt_ref, *, layer_e, hgw):
    h = _rmsnorm(x_ref[...], g_ref[...]).astype(BF16)

    @pl.when(pl.program_id(1) == 0)
    def _():
        carry_ref[...] = jnp.zeros_like(carry_ref)
        carryt_ref[...] = jnp.zeros_like(carryt_ref)

    flf = _log_sigmoid(_dot(h, wff_ref[...]) + fb_ref[...])
    c = sum(_dot(tri_ref[...], p) for p in _split3(flf)) + carry_ref[...]
    carry_ref[...] = c[-1:, :]
    parts = _split3(c * LOG2E)
    caug_ref[...] = sum(_dot(p, place_ref[i]) for i, p in enumerate(parts)).astype(BF16)
    flft = _log_sigmoid(_dot_nt(wfft_ref[...], h) + fbt_ref[...])
    flft_ref[0] = flft
    ct = sum(_dot(p, trit_ref[...]) for p in _split3(flft)) + carryt_ref[...]
    ct_ref[0] = ct
    carryt_ref[...] = ct[:, -1:]

    proj = _inproj_common(h, w_ref, lb_ref, q_ref, fr_ref, v_ref, gate_ref, layer_e=layer_e, hgw=hgw)
    fk_ref[...] = proj(5)
    qkvt = _dot_nt(wqkvt_ref[...], h)
    fqt_ref[0] = qkvt[0:hgw]
    fkt_ref[0] = qkvt[hgw:2 * hgw]
    fvt_ref[0] = qkvt[2 * hgw:]


def _caug_lane(head, part):
    return LANES * (head // 2) + 3 * (head % 2) + part


def _inproj_prompt(x, g, w_main, w_qkvt, w_ff, fb, hg_lb, *, layer_e, batch, seq_len, tm):
    m, d = x.shape
    hgw = w_qkvt.shape[0] // 3
    nfh = w_ff.shape[1]
    nt = seq_len // tm
    tri = np.tril(np.ones((tm, tm), np.float32))
    place = np.zeros((3, nfh, hgw), np.float32)
    for part in range(3):
        for head in range(nfh):
            place[part, head, _caug_lane(head, part)] = 1.0
    row = lambda n: pl.BlockSpec((tm, n), lambda b, t: (b * nt + t, 0))
    tr = lambda n: pl.BlockSpec((1, n, tm), lambda b, t: (b, 0, t))
    wide = jax.ShapeDtypeStruct((m, hgw), F32)
    wide_t = jax.ShapeDtypeStruct((batch, hgw, seq_len), F32)
    narrow_t = jax.ShapeDtypeStruct((batch, nfh, seq_len), F32)
    return pl.pallas_call(
        functools.partial(_inproj_prompt_body, layer_e=layer_e, hgw=hgw),
        grid=(batch, nt),
        in_specs=[row(d), _full((1, d)), _full(w_main.shape), _full(w_qkvt.shape), _full(w_ff.shape),
                  _full((nfh, d)), _full((1, nfh)), _full((nfh, 1)), _full(hg_lb.shape), _full((tm, tm)),
                  _full((tm, tm)), _full(place.shape)],
        out_specs=[row(hgw)] * 4 + [tr(hgw), row(hgw), tr(hgw), tr(hgw), tr(nfh), tr(nfh), row(hgw)],
        out_shape=[wide] * 4 + [wide_t, wide, wide_t, wide_t, narrow_t, narrow_t,
                                jax.ShapeDtypeStruct((m, hgw), BF16)],
        scratch_shapes=[pltpu.VMEM((1, nfh), F32), pltpu.VMEM((nfh, 1), F32)],
        compiler_params=_cparams(("arbitrary", "arbitrary")),
        name="inproj_prompt",
    )(x, g, w_main, w_qkvt, w_ff, w_ff.T, fb, fb.reshape(nfh, 1), hg_lb, jnp.asarray(tri, BF16), jnp.asarray(tri.T, BF16),
      jnp.asarray(place, BF16))


def _inproj_sample_body(x_ref, g_ref, w_ref, wff_ref, fb_ref, lb_ref,
                        q_ref, fr_ref, v_ref, gate_ref, fq_ref, fk_ref, fv_ref, flf_ref, *, layer_e, hgw):
    h = _rmsnorm(x_ref[...], g_ref[...]).astype(BF16)
    proj = _inproj_common(h, w_ref, lb_ref, q_ref, fr_ref, v_ref, gate_ref, layer_e=layer_e, hgw=hgw)
    fq_ref[...] = proj(4)
    fk_ref[...] = proj(5)
    fv_ref[...] = proj(6)
    flf_ref[...] = _log_sigmoid(_dot(h, wff_ref[...]) + fb_ref[...])


def _inproj_sample(x, g, w_main, w_ff, fb, hg_lb, *, layer_e):
    m, d = x.shape
    hgw = w_main.shape[1] // 7
    nfh = w_ff.shape[1]
    wide = jax.ShapeDtypeStruct((m, hgw), F32)
    return pl.pallas_call(
        functools.partial(_inproj_sample_body, layer_e=layer_e, hgw=hgw),
        grid=(1,),
        in_specs=[_full((m, d)), _full((1, d)), _full(w_main.shape), _full(w_ff.shape), _full((1, nfh)),
                  _full(hg_lb.shape)],
        out_specs=[_full((m, hgw))] * 7 + [_full((m, nfh))],
        out_shape=[wide] * 7 + [jax.ShapeDtypeStruct((m, nfh), F32)],
        compiler_params=_cparams(("arbitrary",)),
        name="inproj_sample",
    )(x, g, w_main, w_ff, fb, hg_lb)


def _hgrn_prompt_body(q_ref, fr_ref, v_ref, gate_ref, gn_ref, tri_ref, o_ref, s_ref,
                      st_ref, oi_ref, qb_ref, kvt_ref, dec_ref, *, n_chunks, n_heads, group):
    t = pl.program_id(1)

    @pl.when(t == 0)
    def _():
        st_ref[...] = jnp.zeros_like(st_ref)

    tri = tri_ref[...]
    rows = lax.broadcasted_iota(jnp.int32, (HG_CHUNK, HG_CHUNK), 0)
    cols = lax.broadcasted_iota(jnp.int32, (HG_CHUNK, HG_CHUNK), 1)
    causal = rows >= cols
    gn = gn_ref[...]

    heads = [slice(h * HG_DIM, (h + 1) * HG_DIM) for h in range(n_heads)]

    def local(g, carry):
        cs = [g * group + i for i in range(group)]
        rows_of = [pl.ds(pl.multiple_of(c * HG_CHUNK, HG_CHUNK), HG_CHUNK) for c in cs]
        frs = [fr_ref[rs, :] for rs in rows_of]
        splits = [_split3(jnp.log(fr)) for fr in frs]
        bs = [sum(_dot(tri, p) for p in parts) for parts in splits]
        qrs, krs, kdecs, vs = [], [], [], []
        for c, rs, fr, b in zip(cs, rows_of, frs, bs):
            q = q_ref[rs, :]
            k = 1.0 - fr
            b_last = b[HG_CHUNK - 1:HG_CHUNK, :]
            b_mid = b[HG_CHUNK // 2:HG_CHUNK // 2 + 1, :]
            kdecs.append((k * jnp.exp(b_last - b)).astype(BF16))
            qrs.append((q * jnp.exp(b - b_mid)).astype(BF16))
            krs.append((k * jnp.exp(b_mid - b)).astype(BF16))
            vs.append(v_ref[rs, :])
            qb_ref[c] = (q * jnp.exp(b)).astype(BF16)
            dec_ref[c] = jnp.exp(b_last)
        atts = [[_dot_nt(qr[:, sl], kr[:, sl]) for sl in heads] for qr, kr in zip(qrs, krs)]
        for c, v, kdec in zip(cs, vs, kdecs):
            for h, sl in enumerate(heads):
                kvt_ref[c, h] = _dot(v[:, sl].T.astype(BF16), kdec[:, sl])
        for c, v, att in zip(cs, vs, atts):
            for h, sl in enumerate(heads):
                oi_ref[c, :, sl] = _dot(jnp.where(causal, att[h], 0.0).astype(BF16), v[:, sl].astype(BF16))
        return carry

    def carried(c, carry):
        rs = pl.ds(pl.multiple_of(c * HG_CHUNK, HG_CHUNK), HG_CHUNK)
        gate = gate_ref[rs, :]
        qb = qb_ref[c]
        dec = dec_ref[c]
        for h in range(n_heads):
            sl = slice(h * HG_DIM, (h + 1) * HG_DIM)
            st = st_ref[h]
            o = oi_ref[c, :, sl] + _dot_nt(qb[:, sl], st.astype(BF16))
            st_ref[h] = st * dec[:, sl] + kvt_ref[c, h]
            o = o * lax.rsqrt(jnp.mean(o * o, axis=-1, keepdims=True) + EPS)
            o_ref[rs, sl] = (o * gn[:, sl] * gate[:, sl]).astype(o_ref.dtype)
        return carry

    lax.fori_loop(0, n_chunks // group, local, 0)
    lax.fori_loop(0, n_chunks, carried, 0, unroll=2)

    @pl.when(t == pl.num_programs(1) - 1)
    def _():
        for h in range(n_heads):
            s_ref[0, h] = st_ref[h].T


def _hgrn_prompt(q, fr, v, gate, gn, *, batch, seq_len, tc):
    m, hgw = q.shape
    n_heads = hgw // HG_DIM
    nt = seq_len // tc
    tri = jnp.asarray(np.tril(np.ones((HG_CHUNK, HG_CHUNK), np.float32)), BF16)
    n_chunks = tc // HG_CHUNK
    row = pl.BlockSpec((tc, hgw), lambda b, t: (b * nt + t, 0))
    return pl.pallas_call(
        functools.partial(_hgrn_prompt_body, n_chunks=n_chunks, n_heads=n_heads,
                          group=math.gcd(n_chunks, 4)),
        grid=(batch, nt),
        in_specs=[row, row, row, row, _full((1, hgw)), _full((HG_CHUNK, HG_CHUNK))],
        out_specs=[row, pl.BlockSpec((1, n_heads, HG_DIM, HG_DIM), lambda b, t: (b, 0, 0, 0))],
        out_shape=[jax.ShapeDtypeStruct((m, hgw), BF16),
                   jax.ShapeDtypeStruct((batch, n_heads, HG_DIM, HG_DIM), F32)],
        scratch_shapes=[pltpu.VMEM((n_heads, HG_DIM, HG_DIM), F32),
                        pltpu.VMEM((n_chunks, HG_CHUNK, hgw), F32),
                        pltpu.VMEM((n_chunks, HG_CHUNK, hgw), BF16),
                        pltpu.VMEM((n_chunks, n_heads, HG_DIM, HG_DIM), F32),
                        pltpu.VMEM((n_chunks, 1, hgw), F32)],
        compiler_params=_cparams(("arbitrary", "arbitrary")),
        name="hgrn_prompt",
    )(q, fr, v, gate, gn, tri)


def _hgrn_sample_body(q_ref, fr_ref, v_ref, gate_ref, gn_ref, s_ref, o_ref, so_ref, *, bb, n_heads):
    def col(r):
        return jnp.broadcast_to(r, (HG_DIM, HG_DIM)).T

    for i in range(bb):
        for h in range(n_heads):
            sl = slice(h * HG_DIM, (h + 1) * HG_DIM)
            fr = fr_ref[i:i + 1, sl]
            dec = jnp.exp(jnp.log(fr))
            s_new = col(dec) * s_ref[i, h] + col(1.0 - fr) * v_ref[i:i + 1, sl]
            so_ref[i, h] = s_new
            o = jnp.sum(col(q_ref[i:i + 1, sl]) * s_new, axis=0, keepdims=True)
            o = o * lax.rsqrt(jnp.mean(o * o, axis=-1, keepdims=True) + EPS)
            o_ref[i:i + 1, sl] = (o * gn_ref[:, sl] * gate_ref[i:i + 1, sl]).astype(o_ref.dtype)


def _hgrn_sample(q, fr, v, gate, gn, state, *, bb):
    m, hgw = q.shape
    n_heads = hgw // HG_DIM
    row = pl.BlockSpec((bb, hgw), lambda i: (i, 0))
    st = pl.BlockSpec((bb, n_heads, HG_DIM, HG_DIM), lambda i: (i, 0, 0, 0))
    return pl.pallas_call(
        functools.partial(_hgrn_sample_body, bb=bb, n_heads=n_heads),
        grid=(m // bb,),
        in_specs=[row, row, row, row, _full((1, hgw)), st],
        out_specs=[row, st],
        out_shape=[jax.ShapeDtypeStruct((m, hgw), BF16), jax.ShapeDtypeStruct(state.shape, F32)],
        compiler_params=_cparams(("arbitrary",)),
        name="hgrn_sample",
    )(q, fr, v, gate, gn, state)


def _fox_prompt_body(qi_ref, ki_ref, qt_ref, k_ref, ca_ref, vt_ref, ct_ref, o_ref, m_ref, acc_ref, *, tk):
    j = pl.program_id(2)
    qi = qi_ref[j]
    ki = ki_ref[j]
    tq = 2 * tk

    @pl.when(ki == 0)
    def _():
        m_ref[...] = jnp.full_like(m_ref, -jnp.inf)
        acc_ref[...] = jnp.zeros_like(acc_ref)

    sub = lax.broadcasted_iota(jnp.int32, (LANES, 1), 0)

    def update(units, masked):
        qt = qt_ref[0] * (FOX_DIM ** -0.5 * LOG2E)
        kaug = jnp.concatenate([k_ref[0].astype(BF16), ca_ref[0]], axis=1)
        vt = vt_ref[0]
        vaug = {hh: jnp.where(sub // FOX_DIM == hh, vt, 1.0).astype(BF16) for hh in sorted({u[0] for u in units})}
        keep = (lax.broadcasted_iota(jnp.int32, (tk, tk), 1)
                >= lax.broadcasted_iota(jnp.int32, (tk, tk), 0))
        lanes = lambda half: slice(half * tk, (half + 1) * tk)
        m_prev = {u: m_ref[u[0], :, lanes(u[1])] for u in units}
        acc_prev = {u: acc_ref[u[0], :, lanes(u[1])] for u in units}
        qaug = {}
        for hh in sorted({u[0] for u in units}):
            qm = jnp.where(sub // FOX_DIM == hh, qt, 0.0)
            neg = jnp.where((sub >= 3 * hh) & (sub < 3 * hh + 3), -1.0, 0.0)
            qaug[hh] = jnp.concatenate([qm, jnp.broadcast_to(neg, (LANES, tq))], axis=0).astype(BF16)
        s_all = {u: _dot(kaug, qaug[u[0]][:, lanes(u[1])]) for u in units}
        m_out, acc_out = {}, {}
        for u in units:
            hh, half = u
            s = jnp.where(keep, s_all[u], -jnp.inf) if u in masked else s_all[u]
            ct = ct_ref[0, 0, hh:hh + 1, lanes(half)] * LOG2E
            m_new = jnp.maximum(m_prev[u], jnp.max(s, axis=0, keepdims=True) + ct)
            p = jnp.exp2(s - (m_new - ct))
            alpha = jnp.exp2(m_prev[u] - m_new)
            acc_out[u] = alpha * acc_prev[u] + _dot(vaug[hh], p.astype(BF16))
            m_out[u] = m_new
        for u in units:
            acc_ref[u[0], :, lanes(u[1])] = acc_out[u]
            m_ref[u[0], :, lanes(u[1])] = m_out[u]

    every = ((0, 0), (1, 0), (0, 1), (1, 1))
    upper = ((0, 1), (1, 1))

    @pl.when(ki < 2 * qi)
    def _():
        update(every, ())

    @pl.when(ki == 2 * qi)
    def _():
        update(every, ((0, 0), (1, 0)))

    @pl.when(ki == 2 * qi + 1)
    def _():
        update(upper, upper)
        a0 = acc_ref[0]
        a1 = acc_ref[1]
        o0 = a0 * (1.0 / a0[FOX_DIM:FOX_DIM + 1])
        o1 = a1 * (1.0 / a1[0:1])
        o_ref[0] = jnp.where(sub // FOX_DIM == 0, o0, o1).T.astype(o_ref.dtype)


def _fox_prompt(fqt, fk, caug, fvt, ct, *, batch, seq_len, tk):
    m, fw = fk.shape
    n_pairs = fw // LANES
    tq = 2 * tk
    nq = seq_len // tq
    qi_tab = np.array([q for q in range(nq) for _ in range(2 * q + 2)], np.int32)
    ki_tab = np.array([k for q in range(nq) for k in range(2 * q + 2)], np.int32)
    r3 = lambda a: a.reshape(batch, seq_len, fw)
    by_q = lambda b, p, j, qt, kt: (b, p, qt[j])
    by_k = lambda b, p, j, qt, kt: (b, kt[j], p)
    grid_spec = pltpu.PrefetchScalarGridSpec(
        num_scalar_prefetch=2,
        grid=(batch, n_pairs, len(qi_tab)),
        in_specs=[
            pl.BlockSpec((1, LANES, tq), by_q),
            pl.BlockSpec((1, tk, LANES), by_k),
            pl.BlockSpec((1, tk, LANES), by_k),
            pl.BlockSpec((1, LANES, tk), lambda b, p, j, qt, kt: (b, p, kt[j])),
            pl.BlockSpec((1, 1, 2, tq), lambda b, p, j, qt, kt: (b, p, 0, qt[j])),
        ],
        out_specs=pl.BlockSpec((1, tq, LANES), lambda b, p, j, qt, kt: (b, qt[j], p)),
        scratch_shapes=[pltpu.VMEM((2, 1, tq), F32), pltpu.VMEM((2, LANES, tq), F32)],
    )
    o = pl.pallas_call(
        functools.partial(_fox_prompt_body, tk=tk),
        grid_spec=grid_spec,
        out_shape=jax.ShapeDtypeStruct((batch, seq_len, fw), BF16),
        compiler_params=_cparams(("arbitrary", "arbitrary", "arbitrary")),
        name="fox_prompt",
    )(jnp.asarray(qi_tab), jnp.asarray(ki_tab), fqt, r3(fk), r3(caug), fvt,
      ct.reshape(batch, n_pairs, 2, seq_len))
    return o.reshape(m, fw)


def _fox_sample_stages(k_refs, v_refs, lf_refs, q_ref, kn_ref, vn_ref, cn_ref, o_ref,
                       qcol_ref, bias_ref, s_ref, p_ref, *, n_heads):
    n_pages = len(k_refs)
    page = lf_refs[0].shape[-1]
    width = n_heads * FOX_DIM
    rows = n_pages * n_heads
    hpb = LANES // FOX_DIM
    n_blocks = n_heads // hpb
    st = {}

    def block_of(ref, b):
        return ref[b * LANES:(b + 1) * LANES, :]

    def prepare():
        q_row = q_ref[0] * FOX_DIM ** -0.5
        st["diag"] = (lax.broadcasted_iota(jnp.int32, (n_heads, width), 1) // FOX_DIM
                      == lax.broadcasted_iota(jnp.int32, (n_heads, width), 0))
        st["s_new"] = jnp.sum(jnp.where(st["diag"], q_row, 0.0) * kn_ref[0], axis=-1, keepdims=True)
        for b in range(n_blocks):
            qcol_ref[b] = jnp.broadcast_to(q_row[:, b * LANES:(b + 1) * LANES], (LANES, LANES)).T
        lf = jnp.concatenate([r[...] for r in lf_refs], axis=0)
        pj = lax.broadcasted_iota(jnp.int32, (page, page), 0)
        ps = lax.broadcasted_iota(jnp.int32, (page, page), 1)
        after = jnp.where(pj > ps, 1.0, 0.0).astype(BF16)
        rr = lax.broadcasted_iota(jnp.int32, (rows, rows), 0)
        rc = lax.broadcasted_iota(jnp.int32, (rows, rows), 1)
        later_pages = jnp.where(rc % n_heads == rr % n_heads,
                                jnp.where(rc // n_heads > rr // n_heads, 1.0, 0.0), 0.0).astype(BF16)
        tot = jnp.broadcast_to(jnp.sum(lf, axis=-1, keepdims=True), (rows, page))
        bias = (sum(_dot(p, after) for p in _split3(lf)) + sum(_dot(later_pages, p) for p in _split3(tot))
                + jnp.concatenate([cn_ref[0]] * n_pages, axis=0))
        bias_ref[...] = bias.reshape(n_pages, n_heads, page)

    def scores(b):
        qcol = qcol_ref[b]
        for g in range(n_pages):
            prod = block_of(k_refs[g], b) * qcol
            s_ref[g, b * hpb:(b + 1) * hpb, :] = jnp.sum(prod.reshape(hpb, FOX_DIM, page), axis=1)

    def softmax():
        s = s_ref[...] + bias_ref[...]
        m = jnp.maximum(jnp.max(jnp.max(s, axis=0), axis=-1, keepdims=True), st["s_new"])
        p_new = jnp.exp(st["s_new"] - m)
        p = jnp.exp(s - m)
        inv_l = 1.0 / (p_new + jnp.sum(jnp.sum(p, axis=0), axis=-1, keepdims=True))
        p_ref[...] = p * inv_l
        own_new = jnp.sum(jnp.where(st["diag"], p_new * inv_l, 0.0), axis=0, keepdims=True)
        o_ref[0] = own_new * vn_ref[0]

    def values(b):
        acc = jnp.zeros((LANES, page), F32)
        for g in range(n_pages):
            pb = jnp.broadcast_to(p_ref[g, b * hpb:(b + 1) * hpb, :][:, None, :], (hpb, FOX_DIM, page))
            acc = acc + block_of(v_refs[g], b) * pb.reshape(LANES, page)
        lanes = slice(b * LANES, (b + 1) * LANES)
        o_ref[0, :, lanes] = o_ref[0, :, lanes] + jnp.sum(acc.T, axis=0, keepdims=True)

    return ([prepare] + [functools.partial(scores, b) for b in range(n_blocks)] + [softmax]
            + [functools.partial(values, b) for b in range(n_blocks)])


def _fox_sample_scratch(n_pages, n_heads, page):
    return [pltpu.VMEM((n_heads * FOX_DIM // LANES, LANES, LANES), F32), pltpu.VMEM((n_pages, n_heads, page), F32),
            pltpu.VMEM((n_pages, n_heads, page), F32), pltpu.VMEM((n_pages, n_heads, page), F32)]


def _fox_sample_body(pt_ref, k_hbm, v_hbm, lf_hbm, q_ref, kn_ref, vn_ref, cn_ref, o_ref,
                     qcol_ref, bias_ref, s_ref, p_ref, k_buf, v_buf, lf_buf, sem, *, n_pages, n_heads):
    step = pl.program_id(0)
    last_step = pl.num_programs(0) - 1
    slot = step % 2

    def page_copies(row, to_slot, kind):
        hbm, buf = ((k_hbm, k_buf), (v_hbm, v_buf), (lf_hbm, lf_buf))[kind]
        return [pltpu.make_async_copy(hbm.at[pt_ref[row * n_pages + g]], buf.at[to_slot, g], sem.at[to_slot, kind])
                for g in range(n_pages)]

    def start_row(row, to_slot):
        for kind in range(3):
            for c in page_copies(row, to_slot, kind):
                c.start()

    def wait_row(row, from_slot, kind):
        for c in page_copies(row, from_slot, kind):
            c.wait()

    @pl.when(step == 0)
    def _():
        start_row(0, 0)

    next_row = jnp.minimum(step + 1, last_step)
    start_row(next_row, 1 - slot)

    stages = _fox_sample_stages([k_buf.at[slot, g] for g in range(n_pages)], [v_buf.at[slot, g] for g in range(n_pages)],
                                [lf_buf.at[slot, g] for g in range(n_pages)], q_ref, kn_ref, vn_ref, cn_ref, o_ref,
                                qcol_ref, bias_ref, s_ref, p_ref, n_heads=n_heads)
    first_of = {2: 0, 0: 1, 1: 2 + n_heads * FOX_DIM // LANES}
    for idx, stage in enumerate(stages):
        for kind, first in first_of.items():
            if idx == first:
                wait_row(step, slot, kind)
        stage()

    @pl.when(step == last_step)
    def _():
        for kind in range(3):
            wait_row(next_row, 1 - slot, kind)


def _fox_sample(page_table, cache_kt, cache_vt, cache_lft, fq, fk, fv, flf, *, layer_e):
    bs, n_pages = page_table.shape
    n_layers, n_pool, n_heads, _, page = cache_kt.shape
    assert page == LANES
    width = n_heads * FOX_DIM
    pages_of = lambda c: c.reshape((n_layers * n_pool, -1, page))
    in_hbm = pl.BlockSpec(memory_space=pl.ANY)
    tok = pl.BlockSpec((1, 1, width), lambda b, pt: (b, 0, 0))
    r3 = lambda a: a.reshape(bs, 1, width)
    grid_spec = pltpu.PrefetchScalarGridSpec(
        num_scalar_prefetch=1, grid=(bs,),
        in_specs=[in_hbm, in_hbm, in_hbm, tok, tok, tok, pl.BlockSpec((1, n_heads, 1), lambda b, pt: (b, 0, 0))],
        out_specs=tok,
        scratch_shapes=_fox_sample_scratch(n_pages, n_heads, page)
        + [pltpu.VMEM((2, n_pages, width, page), F32), pltpu.VMEM((2, n_pages, width, page), F32),
           pltpu.VMEM((2, n_pages, n_heads, page), F32), pltpu.SemaphoreType.DMA((2, 3))])
    o = pl.pallas_call(
        functools.partial(_fox_sample_body, n_pages=n_pages, n_heads=n_heads),
        grid_spec=grid_spec,
        out_shape=jax.ShapeDtypeStruct((bs, 1, width), F32),
        compiler_params=_cparams(("arbitrary",)),
        name="fox_sample",
    )((page_table + layer_e * n_pool).reshape(-1), pages_of(cache_kt), pages_of(cache_vt), pages_of(cache_lft),
      r3(fq), r3(fk), r3(fv), flf.reshape(bs, n_heads, 1))
    return o.reshape(bs, width)


def _outproj_body(x_ref, a_ref, b_ref, w_ref, o_ref):
    ka = a_ref.shape[1]
    o_ref[...] = (x_ref[...] + _dot(a_ref[...].astype(BF16), w_ref[:ka, :])
                  + _dot(b_ref[...].astype(BF16), w_ref[ka:, :]))


def _outproj(x, a, b, w, *, tm):
    m, d = x.shape
    row = lambda n: pl.BlockSpec((tm, n), lambda i: (i, 0))
    return pl.pallas_call(
        _outproj_body, grid=(m // tm,),
        in_specs=[row(d), row(a.shape[1]), row(b.shape[1]), _full(w.shape)],
        out_specs=row(d), out_shape=jax.ShapeDtypeStruct((m, d), F32),
        compiler_params=_cparams(("arbitrary",)), name="outproj",
    )(x, a, b, w)


def _gelu(x):
    return 0.5 * x * (1.0 + lax.erf(x * (2.0 ** -0.5)))


def _ffn_cols(ffn_dim):
    cw = 256
    while ffn_dim % cw:
        cw //= 2
    return cw


def _ffn_prompt_body(x_ref, g_ref, win_ref, wdw_ref, bdw_ref, wout_ref, gfin_ref, *refs,
                     tm, ffn_dim, cw, final_norm, mixed):
    mix_refs, refs = (refs[:3], refs[3:]) if mixed else ((), refs)
    o_ref, hist_ref, ext_ref, carry_ref, act_ref = refs
    t = pl.program_id(1)

    @pl.when(t == 0)
    def _():
        carry_ref[...] = jnp.zeros_like(carry_ref)

    x = x_ref[...]
    if mixed:
        a_ref, b_ref, wmix_ref = mix_refs
        ka = a_ref.shape[1]
        x = x + _dot(a_ref[...], wmix_ref[:ka, :]) + _dot(b_ref[...], wmix_ref[ka:, :])
    h = _rmsnorm(x, g_ref[...]).astype(BF16)
    for j in range(ffn_dim // cw):
        cs = slice(j * cw, (j + 1) * cw)
        g = _dot(h, win_ref[:, cs])
        u = _dot(h, win_ref[:, ffn_dim + j * cw:ffn_dim + (j + 1) * cw])
        ext_ref[6:8, :] = carry_ref[6:8, cs]
        ext_ref[8:, :] = g
        carry_ref[6:8, cs] = g[tm - 2:, :]
        gc = (wdw_ref[0:1, cs] * ext_ref[6:6 + tm, :] + wdw_ref[1:2, cs] * ext_ref[7:7 + tm, :]
              + wdw_ref[2:3, cs] * g + bdw_ref[:, cs])
        act_ref[:, cs] = (_gelu(gc) * u).astype(BF16)
    y = x + _dot(act_ref[...], wout_ref[...])
    if final_norm:
        y = _rmsnorm(y, gfin_ref[...])
    o_ref[...] = y

    @pl.when(t == pl.num_programs(1) - 1)
    def _():
        hist_ref[0] = carry_ref[6:8, :]


def _ffn_prompt(x, g, w_in, w_dw, b_dw, w_out, g_final, mix, *, batch, seq_len, tm, final_norm):
    m, d = x.shape
    ffn_dim = w_out.shape[0]
    cw = _ffn_cols(ffn_dim)
    nt = seq_len // tm

    def const(shape):
        return pl.BlockSpec(shape, lambda b, t: (0,) * len(shape), pipeline_mode=pl.Buffered(1))

    rows_of = lambda n: pl.BlockSpec((tm, n), lambda b, t: (b * nt + t, 0))
    mix = () if mix is None else tuple(mix)
    mix_specs = [rows_of(mix[0].shape[1]), rows_of(mix[1].shape[1]), const(mix[2].shape)] if mix else []
    return pl.pallas_call(
        functools.partial(_ffn_prompt_body, tm=tm, ffn_dim=ffn_dim, cw=cw, final_norm=final_norm, mixed=bool(mix)),
        grid=(batch, nt),
        in_specs=[rows_of(d), const((1, d)), const(w_in.shape), const(w_dw.shape), const((1, ffn_dim)),
                  const(w_out.shape), const((1, d))] + mix_specs,
        out_specs=[rows_of(d), pl.BlockSpec((1, 2, ffn_dim), lambda b, t: (b, 0, 0))],
        out_shape=[jax.ShapeDtypeStruct((m, d), F32), jax.ShapeDtypeStruct((batch, 2, ffn_dim), F32)],
        scratch_shapes=[pltpu.VMEM((tm + 8, cw), F32), pltpu.VMEM((8, ffn_dim), F32),
                        pltpu.VMEM((tm, ffn_dim), BF16)],
        compiler_params=_cparams(("arbitrary", "arbitrary")),
        name="ffn_prompt",
    )(x, g, w_in, w_dw, b_dw, w_out, g_final, *mix)


def _ffn_sample_body(x_ref, g_ref, win_ref, wdw_ref, bdw_ref, wout_ref, gfin_ref, hist_ref,
                     o_ref, nh_ref, act_ref, *, ffn_dim, cw, final_norm):
    x = x_ref[...]
    h = _rmsnorm(x, g_ref[...]).astype(BF16)
    for j in range(ffn_dim // cw):
        cs = slice(j * cw, (j + 1) * cw)
        g = _dot(h, win_ref[:, cs])
        u = _dot(h, win_ref[:, ffn_dim + j * cw:ffn_dim + (j + 1) * cw])
        h0 = hist_ref[:, cs]
        h1 = hist_ref[:, ffn_dim + j * cw:ffn_dim + (j + 1) * cw]
        gc = wdw_ref[0:1, cs] * h0 + wdw_ref[1:2, cs] * h1 + wdw_ref[2:3, cs] * g + bdw_ref[:, cs]
        act_ref[:, cs] = (_gelu(gc) * u).astype(BF16)
        nh_ref[:, cs] = h1
        nh_ref[:, ffn_dim + j * cw:ffn_dim + (j + 1) * cw] = g
    y = x + _dot(act_ref[...], wout_ref[...])
    if final_norm:
        y = _rmsnorm(y, gfin_ref[...])
    o_ref[...] = y


def _ffn_sample(x, g, w_in, w_dw, b_dw, w_out, g_final, hist, *, final_norm):
    m, d = x.shape
    ffn_dim = w_out.shape[0]
    cw = _ffn_cols(ffn_dim)
    hist2 = hist.reshape(m, 2 * ffn_dim)
    y, nh = pl.pallas_call(
        functools.partial(_ffn_sample_body, ffn_dim=ffn_dim, cw=cw, final_norm=final_norm),
        grid=(1,),
        in_specs=[_full((m, d)), _full((1, d)), _full(w_in.shape), _full(w_dw.shape),
                  _full((1, ffn_dim)), _full(w_out.shape), _full((1, d)), _full(hist2.shape)],
        out_specs=[_full((m, d)), _full(hist2.shape)],
        out_shape=[jax.ShapeDtypeStruct((m, d), F32), jax.ShapeDtypeStruct(hist2.shape, F32)],
        scratch_shapes=[pltpu.VMEM((m, ffn_dim), BF16)],
        compiler_params=_cparams(("arbitrary",)),
        name="ffn_sample",
    )(x, g, w_in, w_dw, b_dw, w_out, g_final, hist2)
    return y, nh.reshape(m, 2, ffn_dim)


def _layernorm_silu(d, g, b):
    dc = d - jnp.mean(d, axis=-1, keepdims=True)
    y = dc * lax.rsqrt(jnp.mean(dc * dc, axis=-1, keepdims=True) + EPS) * g + b
    return _silu(y)


def _conf_prompt_body(x_ref, g_ref, w1_ref, b1_ref, wdw_ref, bdw_ref, lng_ref, lnb_ref, w2_ref, b2_ref,
                      o_ref, hist_ref, ext_ref, ph_ref, *, tm, ch, cw, pad):
    t = pl.program_id(1)

    @pl.when(t == 0)
    def _():
        ext_ref[0:pad, :] = jnp.zeros((pad, ch), F32)

    x = x_ref[...]
    h = _rmsnorm(x, g_ref[...]).astype(BF16)
    a = _dot(h, w1_ref[:, :ch]) + b1_ref[:, :ch]
    gt = _dot(h, w1_ref[:, ch:]) + b1_ref[:, ch:]
    ext_ref[pad:, :] = a * jax.nn.sigmoid(gt)
    span = ph_ref.shape[1]
    for s in range(1, SUBLANES):
        ph_ref[s - 1] = ext_ref[s:s + span, :]
    off = pad - (cw - 1)
    d = bdw_ref[...]
    for j in range(cw):
        base, s = (off + j) // SUBLANES * SUBLANES, (off + j) % SUBLANES
        src = ext_ref[base:base + tm, :] if s == 0 else ph_ref[s - 1, base:base + tm, :]
        d = d + wdw_ref[j:j + 1, :] * src
    y = _layernorm_silu(d, lng_ref[...], lnb_ref[...]).astype(BF16)
    o_ref[...] = x + _dot(y, w2_ref[...]) + b2_ref[...]

    @pl.when(t == pl.num_programs(1) - 1)
    def _():
        hist_ref[0] = ext_ref[tm + off:tm + pad, :]

    ext_ref[0:pad, :] = ext_ref[tm:tm + pad, :]


def _conf_prompt(x, g, w1, b1, wdw, bdw, lng, lnb, w2, b2, *, batch, seq_len, tm):
    m, d = x.shape
    cw, ch = wdw.shape
    pad = -(-(cw - 1) // SUBLANES) * SUBLANES
    nt = seq_len // tm
    row = pl.BlockSpec((tm, d), lambda b, t: (b * nt + t, 0))
    return pl.pallas_call(
        functools.partial(_conf_prompt_body, tm=tm, ch=ch, cw=cw, pad=pad),
        grid=(batch, nt),
        in_specs=[row, _full((1, d)), _full(w1.shape), _full((1, 2 * ch)), _full(wdw.shape), _full((1, ch)),
                  _full((1, ch)), _full((1, ch)), _full(w2.shape), _full((1, d))],
        out_specs=[row, pl.BlockSpec((1, cw - 1, ch), lambda b, t: (b, 0, 0))],
        out_shape=[jax.ShapeDtypeStruct((m, d), F32), jax.ShapeDtypeStruct((batch, cw - 1, ch), F32)],
        scratch_shapes=[pltpu.VMEM((tm + pad, ch), F32),
                        pltpu.VMEM((SUBLANES - 1, tm + pad - SUBLANES, ch), F32)],
        compiler_params=_cparams(("arbitrary", "arbitrary")),
        name="conf_prompt",
    )(x, g, w1, b1, wdw, bdw, lng, lnb, w2, b2)


def _conf_sample_body(x_ref, g_ref, w1_ref, b1_ref, wdw_ref, bdw_ref, lng_ref, lnb_ref, w2_ref, b2_ref,
                      hist_ref, o_ref, nh_ref, *, ch, cw):
    x = x_ref[...]
    h = _rmsnorm(x, g_ref[...]).astype(BF16)
    a = _dot(h, w1_ref[:, :ch]) + b1_ref[:, :ch]
    gt = _dot(h, w1_ref[:, ch:]) + b1_ref[:, ch:]
    u = a * jax.nn.sigmoid(gt)
    d = bdw_ref[...] + wdw_ref[cw - 1:cw, :] * u
    for j in range(cw - 1):
        d = d + wdw_ref[j:j + 1, :] * hist_ref[j]
    nh_ref[0:cw - 2] = hist_ref[1:cw - 1]
    nh_ref[cw - 2] = u
    y = _layernorm_silu(d, lng_ref[...], lnb_ref[...]).astype(BF16)
    o_ref[...] = x + _dot(y, w2_ref[...]) + b2_ref[...]


def _conf_sample(x, g, w1, b1, wdw, bdw, lng, lnb, w2, b2, hist_t, *, bb):
    m, d = x.shape
    cw, ch = wdw.shape
    row = lambda n: pl.BlockSpec((bb, n), lambda i: (i, 0))
    hs = pl.BlockSpec((cw - 1, bb, ch), lambda i: (0, i, 0))
    return pl.pallas_call(
        functools.partial(_conf_sample_body, ch=ch, cw=cw),
        grid=(m // bb,),
        in_specs=[row(d), _full((1, d)), _full(w1.shape), _full((1, 2 * ch)), _full(wdw.shape), _full((1, ch)),
                  _full((1, ch)), _full((1, ch)), _full(w2.shape), _full((1, d)), hs],
        out_specs=[row(d), hs],
        out_shape=[jax.ShapeDtypeStruct((m, d), F32), jax.ShapeDtypeStruct(hist_t.shape, F32)],
        compiler_params=_cparams(("arbitrary",)),
        name="conf_sample",
    )(x, g, w1, b1, wdw, bdw, lng, lnb, w2, b2, hist_t)


def kernel(x_prompt, x_sample, cache_fox_k, cache_fox_v, cache_fox_logf, page_table, state_hgrn, state_conv,
           state_ffn_conv, norm_mix, norm_ffn, norm_final, w_in0, fox_fb, hg_lb, hg_gnorm, w_out0, w_pw1, b_pw1,
           w_dw, b_dw, ln_g, ln_b, w_pw2, b_pw2, w_ffn_in, w_ffn_dw, b_ffn_dw, w_ffn_out):
    bp, seq_len, d = x_prompt.shape
    bs, dec_seq, _ = x_sample.shape
    assert dec_seq == 1, "the sample path handles one new token per sequence"
    depth = norm_mix.shape[0]
    n_fox_heads = fox_fb.shape[1]
    fox_w = n_fox_heads * FOX_DIM
    hg_w = hg_gnorm.shape[1]
    hg_heads = hg_w // HG_DIM
    assert hg_w == fox_w, "the combined projection is split into equal-width column blocks"
    row2 = lambda a: a.reshape(1, -1)

    xp = x_prompt.reshape(bp * seq_len, d)
    xs = x_sample.reshape(bs, d)
    tm = _tile(seq_len, 512)
    assert depth == 2, "the sample rows' even-layer mixer reads the sample stream before any other layer"

    def conf_args(l):
        o = l // 2
        return (row2(norm_mix[l]), w_pw1[o].astype(BF16), row2(b_pw1[o]), w_dw[o], row2(b_dw[o]),
                row2(ln_g[o]), row2(ln_b[o]), w_pw2[o].astype(BF16), row2(b_pw2[o]))

    def ffn_args(l):
        return (row2(norm_ffn[l]), w_ffn_in[l].astype(BF16), w_ffn_dw[l], row2(b_ffn_dw[l]),
                w_ffn_out[l].astype(BF16), row2(norm_final))

    fk_p, fv_p, fl_p, fk_s, fv_s, fl_s = [], [], [], [], [], []
    hs_p, hs_s, cs_p, cs_s, ffs_p, ffs_s = [], [], [], [], [], []
    for l in range(depth):
        if l % 2 == 0:
            e = l // 2
            w_main = w_in0[e][:, :7 * hg_w].astype(BF16)
            w_qkvt = w_in0[e][:, 4 * hg_w:7 * hg_w].T.astype(BF16)
            w_ff = w_in0[e][:, 7 * hg_w:].astype(BF16)
            w_out = w_out0[e].astype(BF16)
            gn = row2(hg_gnorm[e])
            fb = row2(fox_fb[e])
            q, fr, v, gate, fqt, fk, fkt, fvt, flft, ct, caug = _inproj_prompt(
                xp, row2(norm_mix[l]), w_main, w_qkvt, w_ff, fb, hg_lb, layer_e=e, batch=bp, seq_len=seq_len, tm=tm)
            o_hg, s_fin = _hgrn_prompt(q, fr, v, gate, gn, batch=bp, seq_len=seq_len, tc=tm)
            o_fox = _fox_prompt(fqt, fk, caug, fvt, ct, batch=bp, seq_len=seq_len, tk=_tile(seq_len // 4, 512))
            mix = (o_hg, o_fox, w_out)
            to_heads = lambda a: a.reshape(bp, n_fox_heads, FOX_DIM, seq_len).transpose(0, 3, 1, 2)
            fk_p.append(to_heads(fkt))
            fv_p.append(to_heads(fvt))
            fl_p.append(flft.transpose(0, 2, 1))
            hs_p.append(s_fin)
            q, fr, v, gate, fq, fk, fv, flf = _inproj_sample(
                xs, row2(norm_mix[l]), w_main, w_ff, fb, hg_lb, layer_e=e)
            o_hg_s, s_fin = _hgrn_sample(q, fr, v, gate, gn, state_hgrn[e], bb=8)
            o_fox_s = _fox_sample(page_table, cache_fox_k.transpose(0, 1, 3, 4, 2),
                                  cache_fox_v.transpose(0, 1, 3, 4, 2), cache_fox_logf.transpose(0, 1, 3, 2),
                                  fq, fk, fv, flf, layer_e=e)
            xs = _outproj(xs, o_hg_s, o_fox_s, w_out, tm=bs)
            fk_s.append(fk.reshape(bs, 1, n_fox_heads, FOX_DIM))
            fv_s.append(fv.reshape(bs, 1, n_fox_heads, FOX_DIM))
            fl_s.append(flf.reshape(bs, 1, n_fox_heads))
            hs_s.append(s_fin)
        else:
            xp, hist = _conf_prompt(xp, *conf_args(l), batch=bp, seq_len=seq_len, tm=tm)
            cs_p.append(hist)
        xp, hist = _ffn_prompt(xp, *ffn_args(l), mix if l % 2 == 0 else None,
                               batch=bp, seq_len=seq_len, tm=tm, final_norm=l == depth - 1)
        ffs_p.append(hist)

    for l in range(depth):
        if l % 2 == 1:
            xs, hist_t = _conf_sample(xs, *conf_args(l), state_conv[l // 2].transpose(1, 0, 2), bb=min(bs, 32))
            cs_s.append(hist_t.transpose(1, 0, 2))
        xs, hist = _ffn_sample(xs, *ffn_args(l), state_ffn_conv[l], final_norm=l == depth - 1)
        ffs_s.append(hist)

    return (xp.reshape(bp, seq_len, d), xs.reshape(bs, 1, d),
            jnp.stack(fk_p), jnp.stack(fv_p), jnp.stack(fl_p),
            jnp.stack(fk_s), jnp.stack(fv_s), jnp.stack(fl_s),
            jnp.stack(hs_p), jnp.stack(hs_s), jnp.stack(cs_p), jnp.stack(cs_s),
            jnp.stack(ffs_p), jnp.stack(ffs_s))
```
